```python
import jax, jax.numpy as jnp
from jax import lax
import numpy as np

D_MODEL = 1024
BATCH = 4
SEQ = 4096
DEPTH = 2

CTX_LEN = 256
GRID_W = 64
D_MIX = D_MODEL
D_SSD = D_MIX // 2
SSD_HEAD_DIM = 64
SSD_HEADS = D_SSD // SSD_HEAD_DIM
SSD_GROUPS = 2
SSD_HPG = SSD_HEADS // SSD_GROUPS
SSD_STATE = 128
SSD_CONV = 3
SSD_CHUNK = 128
D_S5 = D_MIX - D_SSD
S5_CH = 16
S5_GROUPS = D_S5 // S5_CH
S5_STATE = 64
FFN_CONV = 3
D_FF = 2816
XBC_DIM = D_SSD + 2 * SSD_GROUPS * SSD_STATE
D_PROJ = D_SSD + XBC_DIM + 2 * SSD_HEADS + D_S5
NORM_EPS = 1e-6

kernel_name = 'hybrid_bidir_ssd_s5_flow_block'


def rmsnorm(x, g):
    x32 = x.astype(jnp.float32)
    y = x32 * lax.rsqrt(jnp.mean(x32 * x32, axis=-1, keepdims=True) + NORM_EPS)
    return (y * g.astype(jnp.float32)).astype(x.dtype)


def modulate(x, shift, scale):
    return x * (1 + scale[:, None, :]) + shift[:, None, :]


def ada_params(cvec, w, b):
    return jnp.split(jax.nn.silu(cvec) @ w + b, 6, axis=-1)


def dwconv_centred(x, w, b):
    k_w = w.shape[0]
    pad = k_w // 2
    length = x.shape[1]
    xp = jnp.pad(x, ((0, 0), (pad, pad), (0, 0)))
    out = xp[:, 0:length] * w[0] + b
    for k in range(1, k_w):
        out = out + xp[:, k:k + length] * w[k]
    return out


def flip(t):
    return jnp.flip(t, axis=1)


def ssd_scan(xh, dt, a, bm, cm, h0, with_output):
    bsz, length = xh.shape[:2]
    q = SSD_CHUNK
    nc = length // q
    xh = xh.reshape(bsz, nc, q, SSD_GROUPS, SSD_HPG, SSD_HEAD_DIM)
    dt = dt.reshape(bsz, nc, q, SSD_GROUPS, SSD_HPG)
    bm = bm.reshape(bsz, nc, q, SSD_GROUPS, SSD_STATE)
    cm = cm.reshape(bsz, nc, q, SSD_GROUPS, SSD_STATE)
    cum = jnp.cumsum(dt * a.reshape(SSD_GROUPS, SSD_HPG), axis=2)
    xdt = xh * dt[..., None]
    to_end = jnp.exp(cum[:, :, -1:] - cum)
    states = jnp.einsum('bcqgn,bcqgep->bcgepn', bm, xdt * to_end[..., None])
    chunk_decay = jnp.exp(cum[:, :, -1])

    def step(h, inp):
        s, d = inp
        return h * d[..., None, None] + s, h

    h_final, h_prev = lax.scan(step, h0, (jnp.moveaxis(states, 1, 0), jnp.moveaxis(chunk_decay, 1, 0)))
    if not with_output:
        return None, h_final
    h_prev = jnp.moveaxis(h_prev, 0, 1)
    tri = jnp.tril(jnp.ones((q, q), dtype=bool))
    seg = cum[:, :, :, None] - cum[:, :, None, :]
    decay = jnp.exp(jnp.where(tri[:, :, None, None], seg, -jnp.inf))
    cb = jnp.einsum('bcign,bcjgn->bcijg', cm, bm)
    y_diag = jnp.einsum('bcijge,bcjgep->bcigep', cb[..., None] * decay, xdt)
    y_off = jnp.einsum('bcign,bcgepn->bcigep', cm, h_prev) * jnp.exp(cum)[..., None]
    y = (y_diag + y_off).reshape(bsz, length, SSD_HEADS, SSD_HEAD_DIM)
    return y, h_final


def s5_scan(bu, lam_bar, h0):
    bu = bu.at[:, 0].add(lam_bar * h0)
    a = jnp.broadcast_to(lam_bar, bu.shape)

    def combine(left, right):
        a_l, b_l = left
        a_r, b_r = right
        return a_r * a_l, a_r * b_l + b_r

    _, hs = lax.associative_scan(combine, (a, bu), axis=1)
    return hs


def token_mixers(h, p, init, with_output):
    f32 = jnp.float32
    bsz, length, _ = h.shape
    proj = h @ p['w_in']
    z, xbc, dt_raw, u = jnp.split(proj, [D_SSD, D_SSD + XBC_DIM, D_SSD + XBC_DIM + 2 * SSD_HEADS], axis=-1)

    xbc = jax.nn.silu(dwconv_centred(xbc, p['ssd_conv_w'], p['ssd_conv_b'])).astype(f32)
    xs, bm, cm = jnp.split(xbc, [D_SSD, D_SSD + SSD_GROUPS * SSD_STATE], axis=-1)
    xh = xs.reshape(bsz, length, SSD_HEADS, SSD_HEAD_DIM)
    bm = bm.reshape(bsz, length, SSD_GROUPS, SSD_STATE)
    cm = cm.reshape(bsz, length, SSD_GROUPS, SSD_STATE)
    dt = jax.nn.softplus(dt_raw.astype(f32).reshape(bsz, length, 2, SSD_HEADS) + p['ssd_dt_bias'].astype(f32))
    a = -jnp.exp(p['ssd_a_log'].astype(f32))
    h_f0, h_b0, s_f0, s_b0 = init
    y_f, hf = ssd_scan(xh, dt[:, :, 0], a[0], bm, cm, h_f0, with_output)
    y_b, hb = ssd_scan(flip(xh), flip(dt[:, :, 1]), a[1], flip(bm), flip(cm), h_b0, with_output)

    lam = lax.complex(p['s5_a_re'].astype(f32), p['s5_a_im'].astype(f32))
    step_size = jnp.exp(p['s5_log_dt'].astype(f32))[..., None]
    lam_bar = jnp.exp(lam * step_size)
    b_c = lax.complex(p['s5_b_re'].astype(f32), p['s5_b_im'].astype(f32))
    b_bar = ((lam_bar - 1) / lam)[..., None] * b_c
    ug = u.astype(f32).reshape(bsz, length, S5_GROUPS, S5_CH)
    bu_f = jnp.einsum('blgc,gpc->blgp', ug, b_bar[0])
    bu_b = jnp.einsum('blgc,gpc->blgp', flip(ug), b_bar[1])
    hs_f = s5_scan(bu_f, lam_bar[0], s_f0)
    hs_b = s5_scan(bu_b, lam_bar[1], s_b0)
    states = (hf, hb, hs_f[:, -1], hs_b[:, -1])
    if not with_output:
        return None, states

    y_ssd = y_f + flip(y_b) + xh * p['ssd_d'].astype(f32)[:, None]
    y_ssd = rmsnorm(y_ssd.reshape(bsz, length, D_SSD) * jax.nn.silu(z.astype(f32)), p['ssd_norm_g'])

    c_c = lax.complex(p['s5_c_re'].astype(f32), p['s5_c_im'].astype(f32))
    y_s5 = jnp.real(jnp.einsum('gcp,blgp->blgc', c_c, hs_f + flip(hs_b)))
    y_s5 = y_s5 + ug * p['s5_d'].astype(f32).reshape(S5_GROUPS, S5_CH)
    g = jax.nn.gelu(y_s5.reshape(bsz, length, D_S5))
    y_s5 = g * jax.nn.sigmoid(g @ p['s5_glu_w'] + p['s5_glu_b'])

    out = jnp.concatenate([y_ssd, y_s5], axis=-1).astype(h.dtype) @ p['w_out']
    return out, states


def conv_ffn(h, w_up, conv_w, conv_b, w_down, rows):
    up = h @ w_up
    bsz, length, ch = up.shape
    if rows is None:
        up = dwconv_centred(up, conv_w, conv_b)
    else:
        up = dwconv_centred(up.reshape(bsz * rows, GRID_W, ch), conv_w, conv_b).reshape(bsz, length, ch)
    gate, val = jnp.split(up, 2, axis=-1)
    return (jax.nn.silu(gate) * val) @ w_down


def setup_inputs(seed: int = 0) -> dict:
    key = jax.random.key(seed)
    ks = jax.random.split(key, 40)
    f32 = jnp.float32

    def nrm(k, shape, s):
        return jax.random.normal(k, shape, f32) * s

    dt0 = jnp.exp(jax.random.uniform(ks[12], (DEPTH, 2, SSD_HEADS), f32, np.log(1e-3), np.log(1e-1)))
    n_idx = jnp.arange(S5_STATE, dtype=f32)
    return {
        'x': nrm(ks[0], (BATCH, SEQ, D_MODEL), 1.0),
        'c': nrm(ks[1], (BATCH, D_MODEL), 1.0),
        'ctx': nrm(ks[2], (BATCH, CTX_LEN, D_MODEL), 1.0),
        'c_ctx': nrm(ks[3], (D_MODEL,), 1.0),
        'w_ada': nrm(ks[4], (DEPTH, D_MODEL, 6 * D_MODEL), D_MODEL ** -0.5),
        'b_ada': nrm(ks[5], (DEPTH, 6 * D_MODEL), 0.02),
        'g_pre_mix': 1.0 + nrm(ks[6], (DEPTH, D_MODEL), 0.05),
        'g_post_mix': 1.0 + nrm(ks[7], (DEPTH, D_MODEL), 0.05),
        'g_pre_ffn': 1.0 + nrm(ks[8], (DEPTH, D_MODEL), 0.05),
        'g_post_ffn': 1.0 + nrm(ks[9], (DEPTH, D_MODEL), 0.05),
        'w_in': nrm(ks[10], (DEPTH, D_MODEL, D_PROJ), D_MODEL ** -0.5),
        'ssd_conv_w': nrm(ks[11], (DEPTH, SSD_CONV, XBC_DIM), SSD_CONV ** -0.5),
        'ssd_conv_b': nrm(ks[13], (DEPTH, XBC_DIM), 0.02),
        'ssd_dt_bias': dt0 + jnp.log(-jnp.expm1(-dt0)),
        'ssd_a_log': jnp.log(jax.random.uniform(ks[14], (DEPTH, 2, SSD_HEADS), f32, 1.0, 16.0)),
        'ssd_d': 1.0 + nrm(ks[15], (DEPTH, SSD_HEADS), 0.05),
        'ssd_norm_g': 1.0 + nrm(ks[16], (DEPTH, D_SSD), 0.05),
        's5_a_re': -0.5 + nrm(ks[17], (DEPTH, 2, S5_GROUPS, S5_STATE), 0.01),
        's5_a_im': jnp.pi * n_idx + nrm(ks[18], (DEPTH, 2, S5_GROUPS, S5_STATE), 0.01),
        's5_log_dt': jax.random.uniform(ks[19], (DEPTH, 2, S5_GROUPS), f32, np.log(1e-3), np.log(1e-1)),
        's5_b_re': nrm(ks[20], (DEPTH, S5_GROUPS, S5_STATE, S5_CH), (2 * S5_CH) ** -0.5),
        's5_b_im': nrm(ks[21], (DEPTH, S5_GROUPS, S5_STATE, S5_CH), (2 * S5_CH) ** -0.5),
        's5_c_re': nrm(ks[22], (DEPTH, S5_GROUPS, S5_CH, S5_STATE), (2 * S5_STATE) ** -0.5),
        's5_c_im': nrm(ks[23], (DEPTH, S5_GROUPS, S5_CH, S5_STATE), (2 * S5_STATE) ** -0.5),
        's5_d': nrm(ks[24], (DEPTH, D_S5), 1.0),
        's5_glu_w': nrm(ks[25], (DEPTH, D_S5, D_S5), D_S5 ** -0.5),
        's5_glu_b': nrm(ks[26], (DEPTH, D_S5), 0.02),
        'w_out': nrm(ks[27], (DEPTH, D_MIX, D_MODEL), D_MIX ** -0.5),
        'ffn_w_up': nrm(ks[28], (DEPTH, D_MODEL, 2 * D_FF), D_MODEL ** -0.5),
        'ffn_conv_w': nrm(ks[29], (DEPTH, FFN_CONV, 2 * D_FF), FFN_CONV ** -0.5),
        'ffn_conv_b': nrm(ks[30], (DEPTH, 2 * D_FF), 0.02),
        'ffn_w_down': nrm(ks[31], (DEPTH, D_FF, D_MODEL), D_FF ** -0.5),
    }


def reference(x, c, ctx, c_ctx, w_ada, b_ada, g_pre_mix, g_post_mix, g_pre_ffn, g_post_ffn,
              w_in, ssd_conv_w, ssd_conv_b, ssd_dt_bias, ssd_a_log, ssd_d, ssd_norm_g,
              s5_a_re, s5_a_im, s5_log_dt, s5_b_re, s5_b_im, s5_c_re, s5_c_im, s5_d, s5_glu_w, s5_glu_b,
              w_out, ffn_w_up, ffn_conv_w, ffn_conv_b, ffn_w_down):
    bsz = x.shape[0]
    rows = x.shape[1] // GRID_W
    for l in range(DEPTH):
        last = l == DEPTH - 1
        p = {
            'w_in': w_in[l], 'ssd_conv_w': ssd_conv_w[l], 'ssd_conv_b': ssd_conv_b[l],
            'ssd_dt_bias': ssd_dt_bias[l], 'ssd_a_log': ssd_a_log[l], 'ssd_d': ssd_d[l],
            'ssd_norm_g': ssd_norm_g[l], 's5_a_re': s5_a_re[l], 's5_a_im': s5_a_im[l],
            's5_log_dt': s5_log_dt[l], 's5_b_re': s5_b_re[l], 's5_b_im': s5_b_im[l],
            's5_c_re': s5_c_re[l], 's5_c_im': s5_c_im[l], 's5_d': s5_d[l],
            's5_glu_w': s5_glu_w[l], 's5_glu_b': s5_glu_b[l], 'w_out': w_out[l],
        }
        sh1, sc1, gt1, sh2, sc2, gt2 = ada_params(c, w_ada[l], b_ada[l])
        csh1, csc1, cgt1, csh2, csc2, cgt2 = ada_params(c_ctx[None, :], w_ada[l], b_ada[l])
        zero_states = (
            jnp.zeros((bsz, SSD_GROUPS, SSD_HPG, SSD_HEAD_DIM, SSD_STATE), jnp.float32),
            jnp.zeros((bsz, SSD_GROUPS, SSD_HPG, SSD_HEAD_DIM, SSD_STATE), jnp.float32),
            jnp.zeros((bsz, S5_GROUPS, S5_STATE), jnp.complex64),
            jnp.zeros((bsz, S5_GROUPS, S5_STATE), jnp.complex64),
        )
        hc = modulate(rmsnorm(ctx, g_pre_mix[l]), csh1, csc1)
        ctx_mix, ctx_states = token_mixers(hc, p, zero_states, not last)
        hx = modulate(rmsnorm(x, g_pre_mix[l]), sh1, sc1)
        x_mix, _ = token_mixers(hx, p, ctx_states, True)
        x = x + gt1[:, None, :] * rmsnorm(x_mix, g_post_mix[l])
        hx = modulate(rmsnorm(x, g_pre_ffn[l]), sh2, sc2)
        x = x + gt2[:, None, :] * rmsnorm(conv_ffn(hx, ffn_w_up[l], ffn_conv_w[l], ffn_conv_b[l], ffn_w_down[l], rows), g_post_ffn[l])
        if not last:
            ctx = ctx + cgt1[:, None, :] * rmsnorm(ctx_mix, g_post_mix[l])
            hc = modulate(rmsnorm(ctx, g_pre_ffn[l]), csh2, csc2)
            ctx = ctx + cgt2[:, None, :] * rmsnorm(conv_ffn(hc, ffn_w_up[l], ffn_conv_w[l], ffn_conv_b[l], ffn_w_down[l], None), g_post_ffn[l])
    return x
```

```python
import functools

import jax
import jax.numpy as jnp
from jax import lax
from jax.experimental import pallas as pl
from jax.experimental.pallas import tpu as pltpu

F32 = jnp.float32
BF16 = jnp.bfloat16

D_MODEL = 1024
D_SSD = 512
SSD_HEADS = 8
SSD_HEAD_DIM = 64
SSD_GROUPS = 2
SSD_STATE = 128
SSD_CHUNK = 128
XBC_DIM = D_SSD + 2 * SSD_GROUPS * SSD_STATE
D_S5 = 512
S5_CH = 16
S5_GROUPS = 32
S5_STATE = 64
S5_Q = 16
S5_PAIRS = S5_GROUPS // 2
D_FF = 2816
FFN_COLS = 256
GRID_W = 64
NORM_EPS = 1e-6
LANES = 128
TOKEN_TILE = 256
W_IN_COLS = D_SSD + XBC_DIM + D_S5 + LANES
MOD_ROWS = 8
VMEM_LIMIT = 56 * 1024 * 1024


def _silu(x):
    return x * (1.0 / (1.0 + jnp.exp(-x)))


def _sigmoid(x):
    return 1.0 / (1.0 + jnp.exp(-x))


def _gelu_tanh(x):
    c = 0.7978845608028654
    return x * (0.5 * (1.0 + jnp.tanh(c * (x + 0.044715 * (x * x * x)))))


def _softplus(x):
    return jnp.maximum(x, 0.0) + jnp.log(1.0 + jnp.exp(-jnp.abs(x)))


def _rms(x, g):
    ms = jnp.mean(x * x, axis=-1, keepdims=True)
    return x * lax.rsqrt(ms + NORM_EPS) * g


def _dot(a, b):
    return jnp.dot(a, b, preferred_element_type=F32)


def _full(shape):
    n = len(shape)
    return pl.BlockSpec(shape, lambda *_: (0,) * n)


def _resident(shape):
    n = len(shape)
    return pl.BlockSpec(shape, lambda *_: (0,) * n, pipeline_mode=pl.Buffered(1))


def _ada_kernel(c_ref, w_ref, b_ref, o_ref):
    c = _silu(c_ref[...]).astype(BF16)
    o_ref[0] = _dot(c, w_ref[0].astype(BF16)) + b_ref[0]


def _ada_table(cvec, w_ada, b_ada):
    depth = w_ada.shape[0]
    n_col = 6 * D_MODEL // D_MODEL
    return pl.pallas_call(
        _ada_kernel,
        grid=(depth, n_col),
        in_specs=[
            _full((MOD_ROWS, D_MODEL)),
            pl.BlockSpec((1, D_MODEL, D_MODEL), lambda l, n: (l, 0, n)),
            pl.BlockSpec((1, 1, D_MODEL), lambda l, n: (l, 0, n)),
        ],
        out_specs=pl.BlockSpec((1, MOD_ROWS, D_MODEL), lambda l, n: (l, 0, n)),
        out_shape=jax.ShapeDtypeStruct((depth, MOD_ROWS, 6 * D_MODEL), F32),
        name="ada_table",
    )(cvec, w_ada, b_ada.reshape(depth, 1, 6 * D_MODEL))


def _inproj_kernel(x_ref, xp_ref, xn_ref, mod_ref, g_ref, w_ref, cw_ref, cb_ref, dtb_ref,
                   z_ref, xbc_ref, dt_ref, u_ref, *, ctx_tiles):
    t = pl.program_id(1)
    nt = pl.num_programs(1)
    tm = x_ref.shape[1]
    g = g_ref[...]
    mod = mod_ref[0]
    shift = mod[:, 0:D_MODEL]
    scale = mod[:, D_MODEL:2 * D_MODEL]

    def prenorm(xv):
        return (_rms(xv, g) * (1.0 + scale) + shift).astype(BF16)

    proj = _dot(prenorm(x_ref[0]), w_ref[...])
    z_ref[0] = proj[:, 0:D_SSD]
    u_ref[0] = proj[:, D_SSD + XBC_DIM:D_SSD + XBC_DIM + D_S5]
    dt_ref[0] = _softplus(proj[:, D_SSD + XBC_DIM + D_S5:] + dtb_ref[...])
    xbc = proj[:, D_SSD:D_SSD + XBC_DIM]

    w_xbc = w_ref[:, D_SSD:D_SSD + XBC_DIM]
    edge_prev = _dot(prenorm(xp_ref[0]), w_xbc)[7:8, :]
    edge_next = _dot(prenorm(xn_ref[0]), w_xbc)[0:1, :]
    prev_ok = jnp.logical_and(t != 0, t != ctx_tiles)
    next_ok = jnp.logical_and(t != ctx_tiles - 1, t != nt - 1)
    edge_prev = jnp.broadcast_to(edge_prev * prev_ok.astype(F32), (8, XBC_DIM))
    edge_next = jnp.broadcast_to(edge_next * next_ok.astype(F32), (8, XBC_DIM))
    r8 = lax.broadcasted_iota(jnp.int32, (8, XBC_DIM), 0)
    left = pltpu.roll(xbc, 1, 0)
    left = jnp.concatenate([jnp.where(r8 == 0, edge_prev, left[0:8]), left[8:]], axis=0)
    right = pltpu.roll(xbc, tm - 1, 0)
    right = jnp.concatenate([right[:tm - 8], jnp.where(r8 == 7, edge_next, right[tm - 8:])], axis=0)
    conv = left * cw_ref[0:1, :] + xbc * cw_ref[1:2, :] + right * cw_ref[2:3, :] + cb_ref[...]
    xbc_ref[0] = _silu(conv)


def _inproj(xu, mod, layer, g_pre, w_in, conv_w, conv_b, dt_bias, ctx_len):
    bsz, t_all, _ = xu.shape
    tm = TOKEN_TILE
    nt = t_all // tm
    ctx_tiles = ctx_len // tm
    rows8 = tm // 8

    def mod_map(b, t):
        return (layer * MOD_ROWS + jnp.where(t < ctx_tiles, bsz, b), 0, 0)

    tok = lambda w: pl.BlockSpec((1, tm, w), lambda b, t: (b, t, 0))
    return pl.pallas_call(
        functools.partial(_inproj_kernel, ctx_tiles=ctx_tiles),
        grid=(bsz, nt),
        in_specs=[
            tok(D_MODEL),
            pl.BlockSpec((1, 8, D_MODEL), lambda b, t: (b, jnp.maximum(t * rows8 - 1, 0), 0)),
            pl.BlockSpec((1, 8, D_MODEL), lambda b, t: (b, jnp.minimum((t + 1) * rows8, t_all // 8 - 1), 0)),
            pl.BlockSpec((1, 1, 6 * D_MODEL), mod_map),
            _full((1, D_MODEL)),
            _resident((D_MODEL, W_IN_COLS)),
            _full((3, XBC_DIM)),
            _full((1, XBC_DIM)),
            _full((1, LANES)),
        ],
        out_specs=[tok(D_SSD), tok(XBC_DIM), tok(LANES), tok(D_S5)],
        out_shape=[
            jax.ShapeDtypeStruct((bsz, t_all, D_SSD), F32),
            jax.ShapeDtypeStruct((bsz, t_all, XBC_DIM), F32),
            jax.ShapeDtypeStruct((bsz, t_all, LANES), F32),
            jax.ShapeDtypeStruct((bsz, t_all, D_S5), F32),
        ],
        compiler_params=pltpu.CompilerParams(vmem_limit_bytes=VMEM_LIMIT),
        name="inproj",
    )(xu, xu, xu, mod, g_pre, w_in, conv_w, conv_b, dt_bias)


def _split3(x):
    hi = x.astype(BF16)
    r1 = x - hi.astype(F32)
    mid = r1.astype(BF16)
    lo = (r1 - mid.astype(F32)).astype(BF16)
    return hi, mid, lo


def _ssd_direction(d, x_ref, dt_ref, y_ref, st_ref, a_row):
    q = SSD_CHUNK
    fwd = d == 0
    ri = lax.broadcasted_iota(jnp.int32, (q, q), 0)
    ci = lax.broadcasted_iota(jnp.int32, (q, q), 1)
    mask = (ri >= ci) if fwd else (ri <= ci)
    tri = mask.astype(BF16)
    lo64 = lax.broadcasted_iota(jnp.int32, (1, LANES), 1) < SSD_HEAD_DIM

    xbc = x_ref[0]
    dt = dt_ref[0]
    hi, mid, lo = _split3(dt * a_row)
    cum = _dot(tri, hi) + _dot(tri, mid) + _dot(tri, lo)
    tot = cum[q - 1:q, :] if fwd else cum[0:1, :]
    wcol = dt * jnp.exp(tot - cum)
    etot = jnp.exp(tot)
    cum_t = cum.T
    dt_t = dt.T
    xs = xbc[:, 0:D_SSD]
    xs_b = xs.astype(BF16)

    new_states = []
    for g in range(SSD_GROUPS):
        bm = xbc[:, D_SSD + g * SSD_STATE:D_SSD + (g + 1) * SSD_STATE]
        cm = xbc[:, D_SSD + (SSD_GROUPS + g) * SSD_STATE:D_SSD + (SSD_GROUPS + g + 1) * SSD_STATE]
        cb = lax.dot_general(cm.astype(BF16), bm.astype(BF16), (((1,), (1,)), ((), ())),
                             preferred_element_type=F32)
        bm_t = bm.T.astype(BF16)
        xw_parts = []
        for pair in (2 * g, 2 * g + 1):
            sl = slice(pair * LANES, (pair + 1) * LANES)
            xs_pair = xs_b[:, sl]
            st_pair = st_ref[d, :, sl].astype(BF16)
            acc = None
            wl = []
            for hh in range(2):
                lane = SSD_HEADS * d + 2 * pair + hh
                cl = jnp.broadcast_to(cum[:, lane:lane + 1], (q, q))
                seg = cl - cum_t[lane:lane + 1, :]
                decay = jnp.exp(jnp.where(mask, seg, -jnp.inf))
                m = cb * decay * dt_t[lane:lane + 1, :]
                cme = cm * jnp.exp(cl)
                lhs = jnp.concatenate([m, cme], axis=1).astype(BF16)
                sel = lo64 if hh == 0 else jnp.logical_not(lo64)
                rhs = jnp.concatenate([jnp.where(sel, xs_pair, jnp.zeros_like(xs_pair)),
                                       jnp.where(sel, st_pair, jnp.zeros_like(st_pair))], axis=0)
                part = _dot(lhs, rhs)
                acc = part if acc is None else acc + part
                wl.append(jnp.broadcast_to(wcol[:, lane:lane + 1], (q, LANES)))
            y_ref[0, :, sl] = acc
            xw_parts.append((xs[:, sl] * jnp.where(lo64, wl[0], wl[1])).astype(BF16))
        new_states.append(_dot(bm_t, jnp.concatenate(xw_parts, axis=1)))

    for pair in range(SSD_HEADS // 2):
        sl = slice(pair * LANES, (pair + 1) * LANES)
        l0 = SSD_HEADS * d + 2 * pair
        dpair = jnp.where(lo64, jnp.broadcast_to(etot[:, l0:l0 + 1], (1, LANES)),
                          jnp.broadcast_to(etot[:, l0 + 1:l0 + 2], (1, LANES)))
        new = new_states[pair // 2][:, (pair % 2) * LANES:(pair % 2 + 1) * LANES]
        st_ref[d, :, sl] = st_ref[d, :, sl] * dpair + new


def _ssd_kernel(xf_ref, dtf_ref, xb_ref, dtb_ref, alog_ref, yf_ref, yb_ref, st_ref):
    @pl.when(pl.program_id(1) == 0)
    def _():
        st_ref[...] = jnp.zeros_like(st_ref)

    a_row = -jnp.exp(alog_ref[...])
    _ssd_direction(0, xf_ref, dtf_ref, yf_ref, st_ref, a_row)
    _ssd_direction(1, xb_ref, dtb_ref, yb_ref, st_ref, a_row)


def _ssd(xbc, dt, a_log_row, ctx_len):
    bsz, t_all, _ = xbc.shape
    q = SSD_CHUNK
    nch = t_all // q
    ncc = ctx_len // q

    def fmap(b, i):
        return (b, i, 0)

    def bmap(b, i):
        return (b, jnp.where(i < ncc, ncc - 1 - i, nch - 1 - (i - ncc)), 0)

    return pl.pallas_call(
        _ssd_kernel,
        grid=(bsz, nch),
        in_specs=[
            pl.BlockSpec((1, q, XBC_DIM), fmap),
            pl.BlockSpec((1, q, LANES), fmap),
            pl.BlockSpec((1, q, XBC_DIM), bmap),
            pl.BlockSpec((1, q, LANES), bmap),
            _full((1, LANES)),
        ],
        out_specs=[pl.BlockSpec((1, q, D_SSD), fmap), pl.BlockSpec((1, q, D_SSD), bmap)],
        out_shape=[jax.ShapeDtypeStruct((bsz, t_all, D_SSD), F32)] * 2,
        scratch_shapes=[pltpu.VMEM((2, SSD_STATE, D_SSD), F32)],
        compiler_params=pltpu.CompilerParams(dimension_semantics=("arbitrary", "arbitrary"),
                                             vmem_limit_bytes=VMEM_LIMIT),
        name="ssd",
    )(xbc, dt, xbc, dt, a_log_row)


def _s5_kernel(u_ref, tt_ref, ws_ref, wo_ref, a_ref, d_ref, y_ref, s_ref, *, n_tiles, ctx_tiles):
    u = u_ref[0]
    ub = u.astype(BF16)
    s_ref[...] = _dot(ub, ws_ref[0])

    lo = lax.broadcasted_iota(jnp.int32, (8, LANES), 0) < 4
    hi = jnp.logical_not(lo)

    def arow(r):
        return jnp.broadcast_to(a_ref[0, r:r + 1, :], (8, LANES))

    def roll4(v):
        return pltpu.roll(v, 4, 0)

    def tile_step(kt, cre, cim, ar, ai, col, first):
        r0 = pl.multiple_of(kt * 8, 8)
        sre = s_ref[pl.ds(r0, 8), col:col + LANES]
        sim = s_ref[pl.ds(r0, 8), col + LANES:col + 2 * LANES]
        h1re = roll4(ar * cre - ai * cim + sre)
        h1im = roll4(ar * cim + ai * cre + sim)
        s_ref[pl.ds(r0, 8), col:col + LANES] = jnp.where(first, cre, h1re)
        s_ref[pl.ds(r0, 8), col + LANES:col + 2 * LANES] = jnp.where(first, cim, h1im)
        t2re = ar * h1re - ai * h1im + sre
        t2im = ar * h1im + ai * h1re + sim
        second = jnp.logical_not(first)
        return jnp.where(second, t2re, roll4(t2re)), jnp.where(second, t2im, roll4(t2im))

    arf, aif, arb, aib = arow(0), arow(1), arow(2), arow(3)

    def body(i, carry):
        fre, fim, bre, bim = carry
        fre, fim = tile_step(i, fre, fim, arf, aif, 0, lo)
        kb = jnp.where(i < ctx_tiles, ctx_tiles - 1 - i, n_tiles - 1 - (i - ctx_tiles))
        bre, bim = tile_step(kb, bre, bim, arb, aib, 2 * LANES, hi)
        return fre, fim, bre, bim

    zero = jnp.zeros((8, LANES), F32)
    lax.fori_loop(0, n_tiles, body, (zero, zero, zero, zero))

    intra = jnp.concatenate([_dot(ub[:, 0:256], tt_ref[0, 0]), _dot(ub[:, 256:512], tt_ref[0, 1])], axis=1)
    y_ref[0] = intra + _dot(s_ref[...].astype(BF16), wo_ref[0]) + u * d_ref[0]


def _s5(u_pairs, tt, ws, wo, a16, dvec, bsz, ctx_len):
    _, rows, width = u_pairs.shape
    n_tiles = rows // 8
    ctx_tiles = (ctx_len // S5_Q) * bsz // 8
    return pl.pallas_call(
        functools.partial(_s5_kernel, n_tiles=n_tiles, ctx_tiles=ctx_tiles),
        grid=(S5_PAIRS,),
        in_specs=[
            pl.BlockSpec((1, rows, width), lambda p: (p, 0, 0)),
            pl.BlockSpec((1, 2, 256, 256), lambda p: (p, 0, 0, 0)),
            pl.BlockSpec((1, width, 4 * LANES), lambda p: (p, 0, 0)),
            pl.BlockSpec((1, 4 * LANES, width), lambda p: (p, 0, 0)),
            pl.BlockSpec((1, 8, LANES), lambda p: (p, 0, 0)),
            pl.BlockSpec((1, 1, width), lambda p: (p, 0, 0)),
        ],
        out_specs=pl.BlockSpec((1, rows, width), lambda p: (p, 0, 0)),
        out_shape=jax.ShapeDtypeStruct(u_pairs.shape, F32),
        scratch_shapes=[pltpu.VMEM((rows, 4 * LANES), F32)],
        compiler_params=pltpu.CompilerParams(vmem_limit_bytes=VMEM_LIMIT),
        name="s5",
    )(u_pairs, tt, ws, wo, a16, dvec)


def _s5_weights(a_re, a_im, log_dt, b_re, b_im, c_re, c_im, d_skip):
    hp = lax.Precision.HIGHEST
    q, g, p, ch = S5_Q, S5_GROUPS, S5_STATE, S5_CH
    lam = lax.complex(a_re.astype(F32), a_im.astype(F32))
    step = jnp.exp(log_dt.astype(F32))[..., None]
    lam_bar = jnp.exp(lam * step)
    b_bar = ((lam_bar - 1) / lam)[..., None] * lax.complex(b_re.astype(F32), b_im.astype(F32))
    c_c = lax.complex(c_re.astype(F32), c_im.astype(F32))
    pw = jnp.concatenate([jnp.ones((2, 1, g, p), lam_bar.dtype),
                          jnp.cumprod(jnp.broadcast_to(lam_bar[:, None], (2, q, g, p)), axis=1)], axis=1)

    cp = c_c[None, None] * pw[:, :q, :, None, :]
    kk = (jnp.einsum('dtgcp,dgpe->dtgce', jnp.real(cp), jnp.real(b_bar), precision=hp)
          - jnp.einsum('dtgcp,dgpe->dtgce', jnp.imag(cp), jnp.imag(b_bar), precision=hp))
    lag = jnp.arange(q)[None, :] - jnp.arange(q)[:, None]
    kf = jnp.where((lag >= 0)[..., None, None, None], kk[0][jnp.clip(lag, 0, q - 1)], 0.0)
    kb = jnp.where((lag <= 0)[..., None, None, None], kk[1][jnp.clip(-lag, 0, q - 1)], 0.0)
    tt = (kf + kb).transpose(2, 0, 4, 1, 3).reshape(S5_PAIRS, 2, q * ch, q * ch)

    wsf = pw[0, q - 1 - jnp.arange(q)][..., None] * b_bar[0][None]
    wsb = pw[1, jnp.arange(q)][..., None] * b_bar[1][None]
    arr = lambda w: w.transpose(1, 0, 3, 2).reshape(g, q * ch, p)
    ws = jnp.stack([jnp.real(arr(wsf)), jnp.imag(arr(wsf)), jnp.real(arr(wsb)), jnp.imag(arr(wsb))], axis=2)
    eye = jnp.eye(2, dtype=F32)
    ws = ws.reshape(S5_PAIRS, 2, q * ch, 4, 1, p) * eye[None, :, None, None, :, None]
    ws = ws.reshape(S5_PAIRS, 2 * q * ch, 4 * 2 * p)

    cf = c_c[None] * pw[0, 1:q + 1][:, :, None, :]
    cbk = c_c[None] * pw[1, q - jnp.arange(q)][:, :, None, :]
    arr2 = lambda w: w.transpose(1, 3, 0, 2).reshape(g, p, q * ch)
    wo = jnp.stack([jnp.real(arr2(cf)), -jnp.imag(arr2(cf)), jnp.real(arr2(cbk)), -jnp.imag(arr2(cbk))], axis=1)
    wo = wo.reshape(S5_PAIRS, 2, 4, p, q * ch).transpose(0, 2, 1, 3, 4)
    wo = wo[:, :, :, :, None, :] * eye[None, None, :, None, :, None]
    wo = wo.reshape(S5_PAIRS, 4 * 2 * p, 2 * q * ch)

    a16 = pw[:, q].reshape(2, S5_PAIRS, 2 * p)
    a16 = jnp.stack([jnp.real(a16[0]), jnp.imag(a16[0]), jnp.real(a16[1]), jnp.imag(a16[1])], axis=1)
    a16 = jnp.concatenate([a16, jnp.zeros_like(a16)], axis=1)

    dvec = jnp.broadcast_to(d_skip.astype(F32).reshape(S5_PAIRS, 2, 1, ch), (S5_PAIRS, 2, q, ch))
    dvec = dvec.reshape(S5_PAIRS, 1, 2 * q * ch)
    return tt.astype(BF16), ws.astype(BF16), wo.astype(BF16), a16, dvec


def _to_pairs(u):
    bsz, t_all, _ = u.shape
    k = t_all // S5_Q
    u6 = u.reshape(bsz, k, S5_Q, S5_PAIRS, 2, S5_CH)
    return u6.transpose(3, 1, 0, 4, 2, 5).reshape(S5_PAIRS, k * bsz, 2 * S5_Q * S5_CH)


def _from_pairs(y, bsz):
    k = y.shape[1] // bsz
    y6 = y.reshape(S5_PAIRS, k, bsz, 2, S5_Q, S5_CH)
    return y6.transpose(2, 1, 4, 0, 3, 5).reshape(bsz, k * S5_Q, D_S5)


def _tail_kernel(x_ref, yf_ref, yb_ref, xs_ref, z_ref, y5_ref, mod_ref, dv_ref, ng_ref, gw_ref, gb_ref,
                 wo_ref, gpm_ref, gpf_ref, wu_ref, cw_ref, cb_ref, wd_ref, gqf_ref, o_ref,
                 *, ctx_tiles, tile_offset):
    t = pl.program_id(1) + tile_offset
    tm = x_ref.shape[1]
    d = D_MODEL
    mod = mod_ref[0]
    gate1, shift2, scale2, gate2 = (mod[:, 2 * d:3 * d], mod[:, 3 * d:4 * d],
                                    mod[:, 4 * d:5 * d], mod[:, 5 * d:6 * d])

    y_ssd = (yf_ref[0] + yb_ref[0] + xs_ref[0] * dv_ref[...]) * _silu(z_ref[0])
    y_ssd = _rms(y_ssd, ng_ref[...])
    g5 = _gelu_tanh(y5_ref[0])
    y_s5 = g5 * _sigmoid(_dot(g5.astype(BF16), gw_ref[...]) + gb_ref[...])
    mix = _dot(y_ssd.astype(BF16), wo_ref[0:D_SSD, :]) + _dot(y_s5.astype(BF16), wo_ref[D_SSD:, :])
    x1 = x_ref[0] + gate1 * _rms(mix, gpm_ref[...])

    h2 = (_rms(x1, gpf_ref[...]) * (1.0 + scale2) + shift2).astype(BF16)
    period = jnp.where(t < ctx_tiles, tm - 1, GRID_W - 1)
    pos = jnp.bitwise_and(lax.broadcasted_iota(jnp.int32, (tm, FFN_COLS), 0), period)
    has_left = pos != 0
    has_right = pos != period

    def conv(v, col):
        left = jnp.where(has_left, pltpu.roll(v, 1, 0), 0.0)
        right = jnp.where(has_right, pltpu.roll(v, tm - 1, 0), 0.0)
        w = cw_ref[:, col:col + FFN_COLS]
        return left * w[0:1, :] + v * w[1:2, :] + right * w[2:3, :] + cb_ref[:, col:col + FFN_COLS]

    acc = jnp.zeros((tm, d), F32)
    for c in range(D_FF // FFN_COLS):
        c0 = c * FFN_COLS
        gate = conv(_dot(h2, wu_ref[:, c0:c0 + FFN_COLS]), c0)
        val = conv(_dot(h2, wu_ref[:, D_FF + c0:D_FF + c0 + FFN_COLS]), D_FF + c0)
        act = (_silu(gate) * val).astype(BF16)
        acc = acc + _dot(act, wd_ref[c0:c0 + FFN_COLS, :])
    o_ref[0] = x1 + gate2 * _rms(acc, gqf_ref[...])


def _tail(xu, yf, yb, xbc, z, y5, mod, layer, p, ctx_len, latent_only):
    bsz, t_all, _ = xu.shape
    tm = TOKEN_TILE
    ctx_tiles = ctx_len // tm
    off = ctx_tiles if latent_only else 0
    nt = t_all // tm - off

    def mod_map(b, t):
        return (layer * MOD_ROWS + jnp.where(t + off < ctx_tiles, bsz, b), 0, 0)

    tok = lambda w: pl.BlockSpec((1, tm, w), lambda b, t: (b, t + off, 0))
    return pl.pallas_call(
        functools.partial(_tail_kernel, ctx_tiles=ctx_tiles, tile_offset=off),
        grid=(bsz, nt),
        in_specs=[
            tok(D_MODEL), tok(D_SSD), tok(D_SSD), tok(D_SSD), tok(D_SSD), tok(D_S5),
            pl.BlockSpec((1, 1, 6 * D_MODEL), mod_map),
            _full((1, D_SSD)), _full((1, D_SSD)),
            _resident((D_S5, D_S5)), _full((1, D_S5)),
            _resident((D_MODEL, D_MODEL)),
            _full((1, D_MODEL)), _full((1, D_MODEL)),
            _resident((D_MODEL, 2 * D_FF)),
            _full((3, 2 * D_FF)), _full((1, 2 * D_FF)),
            _resident((D_FF, D_MODEL)),
            _full((1, D_MODEL)),
        ],
        out_specs=pl.BlockSpec((1, tm, D_MODEL), lambda b, t: (b, t, 0)),
        out_shape=jax.ShapeDtypeStruct((bsz, nt * tm, D_MODEL), F32),
        compiler_params=pltpu.CompilerParams(vmem_limit_bytes=VMEM_LIMIT),
        name="tail",
    )(xu, yf, yb, xbc, z, y5, mod, p['ssd_d'], p['ssd_norm_g'], p['glu_w'], p['glu_b'], p['w_out'],
      p['g_post_mix'], p['g_pre_ffn'], p['w_up'], p['ffn_conv_w'], p['ffn_conv_b'], p['w_down'],
      p['g_post_ffn'])


def _row(v, width=None):
    v = v.astype(F32).reshape(1, -1)
    if width is not None and v.shape[1] < width:
        v = jnp.pad(v, ((0, 0), (0, width - v.shape[1])))
    return v


def kernel(x, c, ctx, c_ctx, w_ada, b_ada, g_pre_mix, g_post_mix, g_pre_ffn, g_post_ffn, w_in, ssd_conv_w,
           ssd_conv_b, ssd_dt_bias, ssd_a_log, ssd_d, ssd_norm_g, s5_a_re, s5_a_im, s5_log_dt, s5_b_re,
           s5_b_im, s5_c_re, s5_c_im, s5_d, s5_glu_w, s5_glu_b, w_out, ffn_w_up, ffn_conv_w, ffn_conv_b,
           ffn_w_down):
    bsz, seq, _ = x.shape
    ctx_len = ctx.shape[1]
    depth = w_in.shape[0]
    assert ctx_len == TOKEN_TILE and seq % TOKEN_TILE == 0 and bsz + 1 <= MOD_ROWS and bsz * 2 == 8

    cvec = jnp.concatenate([c, c_ctx[None, :], jnp.zeros((MOD_ROWS - bsz - 1, D_MODEL), F32)], axis=0)
    mod = _ada_table(cvec, w_ada, b_ada).reshape(depth * MOD_ROWS, 1, 6 * D_MODEL)
    xu = jnp.concatenate([ctx, x], axis=1)

    z0, z1, z2 = D_SSD, D_SSD + XBC_DIM, D_SSD + XBC_DIM + 2 * SSD_HEADS
    for l in range(depth):
        last = l == depth - 1
        wl = w_in[l]
        w_in_l = jnp.concatenate([wl[:, :z0], wl[:, z0:z1], wl[:, z2:], wl[:, z1:z2],
                                  jnp.zeros((D_MODEL, LANES - 2 * SSD_HEADS), F32)], axis=1).astype(BF16)
        z, xbc, dt, u = _inproj(xu, mod, l, _row(g_pre_mix[l]), w_in_l, ssd_conv_w[l].astype(F32),
                                _row(ssd_conv_b[l]), _row(ssd_dt_bias[l], LANES), ctx_len)
        yf, yb = _ssd(xbc, dt, _row(ssd_a_log[l], LANES), ctx_len)
        tt, ws, wo, a16, dvec = _s5_weights(s5_a_re[l], s5_a_im[l], s5_log_dt[l], s5_b_re[l], s5_b_im[l],
                                            s5_c_re[l], s5_c_im[l], s5_d[l])
        y5 = _from_pairs(_s5(_to_pairs(u), tt, ws, wo, a16, dvec, bsz, ctx_len), bsz)
        p = {
            'ssd_d': _row(jnp.repeat(ssd_d[l], SSD_HEAD_DIM)), 'ssd_norm_g': _row(ssd_norm_g[l]),
            'glu_w': s5_glu_w[l].astype(BF16), 'glu_b': _row(s5_glu_b[l]), 'w_out': w_out[l].astype(BF16),
            'g_post_mix': _row(g_post_mix[l]), 'g_pre_ffn': _row(g_pre_ffn[l]),
            'w_up': ffn_w_up[l].astype(BF16), 'ffn_conv_w': ffn_conv_w[l].astype(F32),
            'ffn_conv_b': _row(ffn_conv_b[l]), 'w_down': ffn_w_down[l].astype(BF16),
            'g_post_ffn': _row(g_post_ffn[l]),
        }
        xu = _tail(xu, yf, yb, xbc, z, y5, mod, l, p, ctx_len, latent_only=last)
    return xu
```

```python
import functools

import jax
import jax.numpy as jnp
from jax import lax
from jax.experimental import pallas as pl
from jax.experimental.pallas import tpu as pltpu

F32 = jnp.float32
BF16 = jnp.bfloat16

D_MODEL = 1024
D_SSD = 512
SSD_HEADS = 8
SSD_HEAD_DIM = 64
SSD_GROUPS = 2
SSD_STATE = 128
SSD_CHUNK = 128
XBC_DIM = D_SSD + 2 * SSD_GROUPS * SSD_STATE
D_S5 = 512
S5_CH = 16
S5_GROUPS = 32
S5_STATE = 64
S5_Q = 16
S5_PAIRS = S5_GROUPS // 2
D_FF = 2816
FFN_COLS = 256
GRID_W = 64
NORM_EPS = 1e-6
LANES = 128
TOKEN_TILE = 256
W_IN_COLS = D_SSD + XBC_DIM + D_S5 + LANES
MOD_ROWS = 8
VMEM_LIMIT = 56 * 1024 * 1024


def _silu(x):
    return x * (1.0 / (1.0 + jnp.exp(-x)))


def _sigmoid(x):
    return 1.0 / (1.0 + jnp.exp(-x))


def _gelu_tanh(x):
    c = 0.7978845608028654
    return x * (0.5 * (1.0 + jnp.tanh(c * (x + 0.044715 * (x * x * x)))))


def _softplus(x):
    return jnp.maximum(x, 0.0) + jnp.log(1.0 + jnp.exp(-jnp.abs(x)))


def _rms(x, g):
    ms = jnp.mean(x * x, axis=-1, keepdims=True)
    return x * lax.rsqrt(ms + NORM_EPS) * g


def _dot(a, b):
    return jnp.dot(a, b, preferred_element_type=F32)


def _full(shape):
    n = len(shape)
    return pl.BlockSpec(shape, lambda *_: (0,) * n)


def _resident(shape):
    n = len(shape)
    return pl.BlockSpec(shape, lambda *_: (0,) * n, pipeline_mode=pl.Buffered(1))


def _ada_kernel(c_ref, w_ref, b_ref, o_ref):
    c = _silu(c_ref[...]).astype(BF16)
    o_ref[0] = _dot(c, w_ref[0].astype(BF16)) + b_ref[0]


def _ada_table(cvec, w_ada, b_ada):
    depth = w_ada.shape[0]
    n_col = 6 * D_MODEL // D_MODEL
    return pl.pallas_call(
        _ada_kernel,
        grid=(depth, n_col),
        in_specs=[
            _full((MOD_ROWS, D_MODEL)),
            pl.BlockSpec((1, D_MODEL, D_MODEL), lambda l, n: (l, 0, n)),
            pl.BlockSpec((1, 1, D_MODEL), lambda l, n: (l, 0, n)),
        ],
        out_specs=pl.BlockSpec((1, MOD_ROWS, D_MODEL), lambda l, n: (l, 0, n)),
        out_shape=jax.ShapeDtypeStruct((depth, MOD_ROWS, 6 * D_MODEL), F32),
        name="ada_table",
    )(cvec, w_ada, b_ada.reshape(depth, 1, 6 * D_MODEL))


def _inproj_kernel(*refs, ctx_tiles, split):
    if split:
        c_ref, refs = refs[0], refs[1:]
    (x_ref, xp_ref, xn_ref, mod_ref, g_ref, w_ref, cw_ref, cb_ref, dtb_ref,
     z_ref, xbc_ref, dt_ref, u_ref) = refs
    t = pl.program_id(1)
    nt = pl.num_programs(1)
    tm = x_ref.shape[1]
    x_tile = jnp.where(t < ctx_tiles, c_ref[0], x_ref[0]) if split else x_ref[0]
    g = g_ref[...]
    mod = mod_ref[0]
    shift = mod[:, 0:D_MODEL]
    scale = mod[:, D_MODEL:2 * D_MODEL]

    def prenorm(xv):
        return (_rms(xv, g) * (1.0 + scale) + shift).astype(BF16)

    proj = _dot(prenorm(x_tile), w_ref[...])
    z_ref[0] = proj[:, 0:D_SSD]
    u_ref[...] = proj[:, D_SSD + XBC_DIM:D_SSD + XBC_DIM + D_S5].reshape(tm // S5_Q, S5_Q, D_S5)
    dt_ref[0] = _softplus(proj[:, D_SSD + XBC_DIM + D_S5:] + dtb_ref[...])
    xbc = proj[:, D_SSD:D_SSD + XBC_DIM]

    w_xbc = w_ref[:, D_SSD:D_SSD + XBC_DIM]
    edge_prev = _dot(prenorm(xp_ref[0]), w_xbc)[7:8, :]
    edge_next = _dot(prenorm(xn_ref[0]), w_xbc)[0:1, :]
    prev_ok = jnp.logical_and(t != 0, t != ctx_tiles)
    next_ok = jnp.logical_and(t != ctx_tiles - 1, t != nt - 1)
    edge_prev = jnp.broadcast_to(edge_prev * prev_ok.astype(F32), (8, XBC_DIM))
    edge_next = jnp.broadcast_to(edge_next * next_ok.astype(F32), (8, XBC_DIM))
    r8 = lax.broadcasted_iota(jnp.int32, (8, XBC_DIM), 0)
    left = pltpu.roll(xbc, 1, 0)
    left = jnp.concatenate([jnp.where(r8 == 0, edge_prev, left[0:8]), left[8:]], axis=0)
    right = pltpu.roll(xbc, tm - 1, 0)
    right = jnp.concatenate([right[:tm - 8], jnp.where(r8 == 7, edge_next, right[tm - 8:])], axis=0)
    conv = left * cw_ref[0:1, :] + xbc * cw_ref[1:2, :] + right * cw_ref[2:3, :] + cb_ref[...]
    xbc_ref[0] = _silu(conv)


def _token_specs(xa, ctx, ctx_tiles, tm):
    if ctx is None:
        return [pl.BlockSpec((1, tm, D_MODEL), lambda b, t: (b, t, 0))], [xa], 0
    specs = [pl.BlockSpec((1, tm, D_MODEL), lambda b, t: (b, 0, 0)),
             pl.BlockSpec((1, tm, D_MODEL), lambda b, t: (b, jnp.maximum(t - ctx_tiles, 0), 0))]
    return specs, [ctx, xa], ctx_tiles


def _inproj(xa, ctx, mod, layer, g_pre, w_in, conv_w, conv_b, dt_bias, ctx_len):
    bsz = xa.shape[0]
    tm = TOKEN_TILE
    ctx_tiles = ctx_len // tm
    x_specs, x_args, lat0 = _token_specs(xa, ctx, ctx_tiles, tm)
    t_all = xa.shape[1] + (ctx_len if ctx is not None else 0)
    nt = t_all // tm
    rows8 = tm // 8
    last8 = xa.shape[1] // 8 - 1
    n_chunks = tm // S5_Q

    def mod_map(b, t):
        return (layer * MOD_ROWS + jnp.where(t < ctx_tiles, bsz, b), 0, 0)

    tok = lambda w: pl.BlockSpec((1, tm, w), lambda b, t: (b, t, 0))
    return pl.pallas_call(
        functools.partial(_inproj_kernel, ctx_tiles=ctx_tiles, split=ctx is not None),
        grid=(bsz, nt),
        in_specs=x_specs + [
            pl.BlockSpec((1, 8, D_MODEL), lambda b, t: (b, jnp.clip((t - lat0) * rows8 - 1, 0, last8), 0)),
            pl.BlockSpec((1, 8, D_MODEL), lambda b, t: (b, jnp.clip((t - lat0 + 1) * rows8, 0, last8), 0)),
            pl.BlockSpec((1, 1, 6 * D_MODEL), mod_map),
            _full((1, D_MODEL)),
            _resident((D_MODEL, W_IN_COLS)),
            _full((3, XBC_DIM)),
            _full((1, XBC_DIM)),
            _full((1, LANES)),
        ],
        out_specs=[tok(D_SSD), tok(XBC_DIM), tok(LANES),
                   pl.BlockSpec((n_chunks, S5_Q, D_S5), lambda b, t: (t, b, 0))],
        out_shape=[
            jax.ShapeDtypeStruct((bsz, t_all, D_SSD), F32),
            jax.ShapeDtypeStruct((bsz, t_all, XBC_DIM), F32),
            jax.ShapeDtypeStruct((bsz, t_all, LANES), F32),
            jax.ShapeDtypeStruct((t_all // S5_Q, bsz * S5_Q, D_S5), F32),
        ],
        compiler_params=pltpu.CompilerParams(vmem_limit_bytes=VMEM_LIMIT),
        name="inproj",
    )(*x_args, xa, xa, mod, g_pre, w_in, conv_w, conv_b, dt_bias)


def _split3(x):
    hi = x.astype(BF16)
    r1 = x - hi.astype(F32)
    mid = r1.astype(BF16)
    lo = (r1 - mid.astype(F32)).astype(BF16)
    return hi, mid, lo


def _ssd_direction(d, x_ref, dt_ref, y_ref, st_ref, a_row):
    q = SSD_CHUNK
    fwd = d == 0
    ri = lax.broadcasted_iota(jnp.int32, (q, q), 0)
    ci = lax.broadcasted_iota(jnp.int32, (q, q), 1)
    mask = (ri >= ci) if fwd else (ri <= ci)
    tri = mask.astype(BF16)
    lo64 = lax.broadcasted_iota(jnp.int32, (1, LANES), 1) < SSD_HEAD_DIM

    xbc = x_ref[0]
    dt = dt_ref[0]
    hi, mid, lo = _split3(dt * a_row)
    cum = _dot(tri, hi) + _dot(tri, mid) + _dot(tri, lo)
    tot = cum[q - 1:q, :] if fwd else cum[0:1, :]
    wcol = dt * jnp.exp(tot - cum)
    etot = jnp.exp(tot)
    cum_t = cum.T
    dt_t = dt.T
    xs = xbc[:, 0:D_SSD]
    xs_b = xs.astype(BF16)

    new_states = []
    for g in range(SSD_GROUPS):
        bm = xbc[:, D_SSD + g * SSD_STATE:D_SSD + (g + 1) * SSD_STATE]
        cm = xbc[:, D_SSD + (SSD_GROUPS + g) * SSD_STATE:D_SSD + (SSD_GROUPS + g + 1) * SSD_STATE]
        cb = lax.dot_general(cm.astype(BF16), bm.astype(BF16), (((1,), (1,)), ((), ())),
                             preferred_element_type=F32)
        bm_t = bm.T.astype(BF16)
        xw_parts = []
        for pair in (2 * g, 2 * g + 1):
            sl = slice(pair * LANES, (pair + 1) * LANES)
            xs_pair = xs_b[:, sl]
            st_pair = st_ref[d, :, sl].astype(BF16)
            acc = None
            wl = []
            for hh in range(2):
                lane = SSD_HEADS * d + 2 * pair + hh
                cl = jnp.broadcast_to(cum[:, lane:lane + 1], (q, q))
                seg = cl - cum_t[lane:lane + 1, :]
                decay = jnp.exp(jnp.where(mask, seg, -jnp.inf))
                m = cb * decay * dt_t[lane:lane + 1, :]
                cme = cm * jnp.exp(cl)
                lhs = jnp.concatenate([m, cme], axis=1).astype(BF16)
                sel = lo64 if hh == 0 else jnp.logical_not(lo64)
                rhs = jnp.concatenate([jnp.where(sel, xs_pair, jnp.zeros_like(xs_pair)),
                                       jnp.where(sel, st_pair, jnp.zeros_like(st_pair))], axis=0)
                part = _dot(lhs, rhs)
                acc = part if acc is None else acc + part
                wl.append(jnp.broadcast_to(wcol[:, lane:lane + 1], (q, LANES)))
            y_ref[0, :, sl] = acc
            xw_parts.append((xs[:, sl] * jnp.where(lo64, wl[0], wl[1])).astype(BF16))
        new_states.append(_dot(bm_t, jnp.concatenate(xw_parts, axis=1)))

    for pair in range(SSD_HEADS // 2):
        sl = slice(pair * LANES, (pair + 1) * LANES)
        l0 = SSD_HEADS * d + 2 * pair
        dpair = jnp.where(lo64, jnp.broadcast_to(etot[:, l0:l0 + 1], (1, LANES)),
                          jnp.broadcast_to(etot[:, l0 + 1:l0 + 2], (1, LANES)))
        new = new_states[pair // 2][:, (pair % 2) * LANES:(pair % 2 + 1) * LANES]
        st_ref[d, :, sl] = st_ref[d, :, sl] * dpair + new


def _ssd_kernel(xf_ref, dtf_ref, xb_ref, dtb_ref, alog_ref, yf_ref, yb_ref, st_ref):
    @pl.when(pl.program_id(1) == 0)
    def _():
        st_ref[...] = jnp.zeros_like(st_ref)

    a_row = -jnp.exp(alog_ref[...])
    _ssd_direction(0, xf_ref, dtf_ref, yf_ref, st_ref, a_row)
    _ssd_direction(1, xb_ref, dtb_ref, yb_ref, st_ref, a_row)


def _ssd(xbc, dt, a_log_row, ctx_len):
    bsz, t_all, _ = xbc.shape
    q = SSD_CHUNK
    nch = t_all // q
    ncc = ctx_len // q

    def fmap(b, i):
        return (b, i, 0)

    def bmap(b, i):
        return (b, jnp.where(i < ncc, ncc - 1 - i, nch - 1 - (i - ncc)), 0)

    return pl.pallas_call(
        _ssd_kernel,
        grid=(bsz, nch),
        in_specs=[
            pl.BlockSpec((1, q, XBC_DIM), fmap),
            pl.BlockSpec((1, q, LANES), fmap),
            pl.BlockSpec((1, q, XBC_DIM), bmap),
            pl.BlockSpec((1, q, LANES), bmap),
            _full((1, LANES)),
        ],
        out_specs=[pl.BlockSpec((1, q, D_SSD), fmap), pl.BlockSpec((1, q, D_SSD), bmap)],
        out_shape=[jax.ShapeDtypeStruct((bsz, t_all, D_SSD), F32)] * 2,
        scratch_shapes=[pltpu.VMEM((2, SSD_STATE, D_SSD), F32)],
        compiler_params=pltpu.CompilerParams(dimension_semantics=("arbitrary", "arbitrary"),
                                             vmem_limit_bytes=VMEM_LIMIT),
        name="ssd",
    )(xbc, dt, xbc, dt, a_log_row)


def _block_transpose8(xs):
    lane = lax.broadcasted_iota(jnp.int32, (1, LANES), 1)
    xs = list(xs)
    for d in (4, 2, 1):
        keep = jnp.bitwise_and(lane, S5_CH * d) == 0
        nxt = list(xs)
        for a in range(8):
            if a & d:
                continue
            nxt[a] = jnp.where(keep, xs[a], pltpu.roll(xs[a + d], S5_CH * d, 1))
            nxt[a + d] = jnp.where(keep, pltpu.roll(xs[a], LANES - S5_CH * d, 1), xs[a + d])
        xs = nxt
    return xs


def _s5_kernel(u_ref, tt_ref, ws_ref, wo_ref, a_ref, d_ref, y_ref, s_ref, ub_ref,
               *, rows_blk, n_tiles, ctx_tiles):
    phase = pl.program_id(1)
    rb = pl.program_id(2)
    q = S5_Q
    gw = q * S5_CH
    pw = 2 * gw
    n_pairs = u_ref.shape[1] // (2 * S5_CH)
    r0 = pl.multiple_of(rb * rows_blk, 8)

    def token_rows(ref, j):
        return ref[pl.ds(j, rows_blk, stride=q), :]

    @pl.when(phase == 0)
    def _():
        uj = [token_rows(u_ref, j) for j in range(q)]
        ug = [_block_transpose8(uj[0:8]), _block_transpose8(uj[8:16])]
        for p in range(n_pairs):
            up = jnp.concatenate([ug[0][2 * p], ug[1][2 * p], ug[0][2 * p + 1], ug[1][2 * p + 1]],
                                 axis=1).astype(BF16)
            ub_ref[pl.ds(r0, rows_blk), p * pw:(p + 1) * pw] = up
            s_ref[pl.ds(r0, rows_blk), p * pw:(p + 1) * pw] = _dot(up, ws_ref[0, p])

    lo = lax.broadcasted_iota(jnp.int32, (8, LANES), 0) < 4
    hi = jnp.logical_not(lo)

    def roll4(v):
        return pltpu.roll(v, 4, 0)

    def tile_step(kt, cre, cim, ar, ai, col, first):
        r0 = pl.multiple_of(kt * 8, 8)
        sre = s_ref[pl.ds(r0, 8), col:col + LANES]
        sim = s_ref[pl.ds(r0, 8), col + LANES:col + 2 * LANES]
        h1re = roll4(ar * cre - ai * cim + sre)
        h1im = roll4(ar * cim + ai * cre + sim)
        s_ref[pl.ds(r0, 8), col:col + LANES] = jnp.where(first, cre, h1re)
        s_ref[pl.ds(r0, 8), col + LANES:col + 2 * LANES] = jnp.where(first, cim, h1im)
        t2re = ar * h1re - ai * h1im + sre
        t2im = ar * h1im + ai * h1re + sim
        second = jnp.logical_not(first)
        return jnp.where(second, t2re, roll4(t2re)), jnp.where(second, t2im, roll4(t2im))

    @pl.when(jnp.logical_and(phase == 1, rb == 0))
    def _():
        def arow(p, r):
            return jnp.broadcast_to(a_ref[0, p, r:r + 1, :], (8, LANES))

        coef = [[arow(p, r) for r in range(4)] for p in range(n_pairs)]

        def body(i, carry):
            kb = jnp.where(i < ctx_tiles, ctx_tiles - 1 - i, n_tiles - 1 - (i - ctx_tiles))
            out = []
            for p in range(n_pairs):
                fre, fim, bre, bim = carry[4 * p:4 * p + 4]
                arf, aif, arb, aib = coef[p]
                fre, fim = tile_step(i, fre, fim, arf, aif, p * pw, lo)
                bre, bim = tile_step(kb, bre, bim, arb, aib, p * pw + 2 * LANES, hi)
                out += [fre, fim, bre, bim]
            return tuple(out)

        zero = jnp.zeros((8, LANES), F32)
        lax.fori_loop(0, n_tiles, body, (zero,) * (4 * n_pairs))

    @pl.when(phase == 1)
    def _():
        yg = [[None] * 8, [None] * 8]
        for p in range(n_pairs):
            up = ub_ref[pl.ds(r0, rows_blk), p * pw:(p + 1) * pw]
            h = s_ref[pl.ds(r0, rows_blk), p * pw:(p + 1) * pw].astype(BF16)
            off = _dot(h, wo_ref[0, p])
            for gi in range(2):
                g = 2 * p + gi
                yp = _dot(up[:, gi * gw:(gi + 1) * gw], tt_ref[0, g]) + off[:, gi * gw:(gi + 1) * gw]
                yg[0][g] = yp[:, 0:LANES]
                yg[1][g] = yp[:, LANES:2 * LANES]
        for half in range(2):
            yi = _block_transpose8(yg[half])
            for il in range(8):
                i = 8 * half + il
                y_ref[pl.ds(i, rows_blk, stride=q), :] = yi[il] + token_rows(u_ref, i) * d_ref[0]


def _s5(u2d, tt, ws, wo, a16, dvec, bsz, ctx_len):
    n_rows, width = u2d.shape
    rows = n_rows // S5_Q
    n_sg = width // LANES
    n_blk = 4
    rows_blk = rows // n_blk
    assert rows_blk % 8 == 0 and (ctx_len // S5_Q) * bsz <= rows_blk
    n_tiles = rows // 8
    ctx_tiles = (ctx_len // S5_Q) * bsz // 8
    pairs_sg = S5_PAIRS // n_sg
    pw = 2 * S5_Q * S5_CH
    return pl.pallas_call(
        functools.partial(_s5_kernel, rows_blk=rows_blk, n_tiles=n_tiles, ctx_tiles=ctx_tiles),
        grid=(n_sg, 2, n_blk),
        in_specs=[
            pl.BlockSpec((rows_blk * S5_Q, LANES), lambda s, ph, rb: (rb, s)),
            pl.BlockSpec((1, 2 * pairs_sg, pw // 2, pw // 2), lambda s, ph, rb: (s, 0, 0, 0)),
            pl.BlockSpec((1, pairs_sg, pw, pw), lambda s, ph, rb: (s, 0, 0, 0)),
            pl.BlockSpec((1, pairs_sg, pw, pw), lambda s, ph, rb: (s, 0, 0, 0)),
            pl.BlockSpec((1, pairs_sg, 8, LANES), lambda s, ph, rb: (s, 0, 0, 0)),
            pl.BlockSpec((1, 1, LANES), lambda s, ph, rb: (s, 0, 0)),
        ],
        out_specs=pl.BlockSpec((rows_blk * S5_Q, LANES), lambda s, ph, rb: (ph * rb, s)),
        out_shape=jax.ShapeDtypeStruct(u2d.shape, F32),
        scratch_shapes=[pltpu.VMEM((rows, pairs_sg * pw), F32), pltpu.VMEM((rows, pairs_sg * pw), BF16)],
        compiler_params=pltpu.CompilerParams(dimension_semantics=("arbitrary", "arbitrary", "arbitrary"),
                                             vmem_limit_bytes=VMEM_LIMIT),
        name="s5",
    )(u2d, tt, ws, wo, a16, dvec)


def _s5_weights(a_re, a_im, log_dt, b_re, b_im, c_re, c_im, d_skip):
    hp = lax.Precision.HIGHEST
    q, g, p, ch = S5_Q, S5_GROUPS, S5_STATE, S5_CH
    lam = lax.complex(a_re.astype(F32), a_im.astype(F32))
    step = jnp.exp(log_dt.astype(F32))[..., None]
    lam_bar = jnp.exp(lam * step)
    b_bar = ((lam_bar - 1) / lam)[..., None] * lax.complex(b_re.astype(F32), b_im.astype(F32))
    c_c = lax.complex(c_re.astype(F32), c_im.astype(F32))
    pw = jnp.concatenate([jnp.ones((2, 1, g, p), lam_bar.dtype),
                          jnp.cumprod(jnp.broadcast_to(lam_bar[:, None], (2, q, g, p)), axis=1)], axis=1)

    cp = c_c[None, None] * pw[:, :q, :, None, :]
    kk = (jnp.einsum('dtgcp,dgpe->dtgce', jnp.real(cp), jnp.real(b_bar), precision=hp)
          - jnp.einsum('dtgcp,dgpe->dtgce', jnp.imag(cp), jnp.imag(b_bar), precision=hp))
    lag = jnp.arange(q)[None, :] - jnp.arange(q)[:, None]
    kf = jnp.where((lag >= 0)[..., None, None, None], kk[0][jnp.clip(lag, 0, q - 1)], 0.0)
    kb = jnp.where((lag <= 0)[..., None, None, None], kk[1][jnp.clip(-lag, 0, q - 1)], 0.0)
    tt = (kf + kb).transpose(2, 0, 4, 1, 3).reshape(S5_PAIRS, 2, q * ch, q * ch)

    wsf = pw[0, q - 1 - jnp.arange(q)][..., None] * b_bar[0][None]
    wsb = pw[1, jnp.arange(q)][..., None] * b_bar[1][None]
    arr = lambda w: w.transpose(1, 0, 3, 2).reshape(g, q * ch, p)
    ws = jnp.stack([jnp.real(arr(wsf)), jnp.imag(arr(wsf)), jnp.real(arr(wsb)), jnp.imag(arr(wsb))], axis=2)
    eye = jnp.eye(2, dtype=F32)
    ws = ws.reshape(S5_PAIRS, 2, q * ch, 4, 1, p) * eye[None, :, None, None, :, None]
    ws = ws.reshape(S5_PAIRS, 2 * q * ch, 4 * 2 * p)

    cf = c_c[None] * pw[0, 1:q + 1][:, :, None, :]
    cbk = c_c[None] * pw[1, q - jnp.arange(q)][:, :, None, :]
    arr2 = lambda w: w.transpose(1, 3, 0, 2).reshape(g, p, q * ch)
    wo = jnp.stack([jnp.real(arr2(cf)), -jnp.imag(arr2(cf)), jnp.real(arr2(cbk)), -jnp.imag(arr2(cbk))], axis=1)
    wo = wo.reshape(S5_PAIRS, 2, 4, p, q * ch).transpose(0, 2, 1, 3, 4)
    wo = wo[:, :, :, :, None, :] * eye[None, None, :, None, :, None]
    wo = wo.reshape(S5_PAIRS, 4 * 2 * p, 2 * q * ch)

    a16 = pw[:, q].reshape(2, S5_PAIRS, 2 * p)
    a16 = jnp.stack([jnp.real(a16[0]), jnp.imag(a16[0]), jnp.real(a16[1]), jnp.imag(a16[1])], axis=1)
    a16 = jnp.concatenate([a16, jnp.zeros_like(a16)], axis=1)

    n_sg = D_S5 // LANES
    psg = S5_PAIRS // n_sg
    tt = tt.astype(BF16).reshape(n_sg, 2 * psg, q * ch, q * ch)
    ws = ws.astype(BF16).reshape(n_sg, psg, 2 * q * ch, 8 * p)
    wo = wo.astype(BF16).reshape(n_sg, psg, 8 * p, 2 * q * ch)
    return tt, ws, wo, a16.reshape(n_sg, psg, 8, 2 * p), d_skip.astype(F32).reshape(n_sg, 1, LANES)


def _tail_kernel(*refs, ctx_tiles, tile_offset, split):
    if split:
        c_ref, refs = refs[0], refs[1:]
    (x_ref, yf_ref, yb_ref, xs_ref, z_ref, y5_ref, mod_ref, dv_ref, ng_ref, gw_ref, gb_ref,
     wo_ref, gpm_ref, gpf_ref, wu_ref, cw_ref, cb_ref, wd_ref, gqf_ref, o_ref) = refs
    t = pl.program_id(1) + tile_offset
    tm = x_ref.shape[1]
    d = D_MODEL
    x_tile = jnp.where(t < ctx_tiles, c_ref[0], x_ref[0]) if split else x_ref[0]
    mod = mod_ref[0]
    gate1, shift2, scale2, gate2 = (mod[:, 2 * d:3 * d], mod[:, 3 * d:4 * d],
                                    mod[:, 4 * d:5 * d], mod[:, 5 * d:6 * d])

    y_ssd = (yf_ref[0] + yb_ref[0] + xs_ref[0] * dv_ref[...]) * _silu(z_ref[0])
    y_ssd = _rms(y_ssd, ng_ref[...])
    g5 = _gelu_tanh(y5_ref[...].reshape(tm, D_S5))
    y_s5 = g5 * _sigmoid(_dot(g5.astype(BF16), gw_ref[...]) + gb_ref[...])
    mix = _dot(y_ssd.astype(BF16), wo_ref[0:D_SSD, :]) + _dot(y_s5.astype(BF16), wo_ref[D_SSD:, :])
    x1 = x_tile + gate1 * _rms(mix, gpm_ref[...])

    h2 = (_rms(x1, gpf_ref[...]) * (1.0 + scale2) + shift2).astype(BF16)
    period = jnp.where(t < ctx_tiles, tm - 1, GRID_W - 1)
    pos = jnp.bitwise_and(lax.broadcasted_iota(jnp.int32, (tm, FFN_COLS), 0), period)
    has_left = pos != 0
    has_right = pos != period

    def conv(v, col):
        left = jnp.where(has_left, pltpu.roll(v, 1, 0), 0.0)
        right = jnp.where(has_right, pltpu.roll(v, tm - 1, 0), 0.0)
        w = cw_ref[:, col:col + FFN_COLS]
        return left * w[0:1, :] + v * w[1:2, :] + right * w[2:3, :] + cb_ref[:, col:col + FFN_COLS]

    acc = jnp.zeros((tm, d), F32)
    for c in range(D_FF // FFN_COLS):
        c0 = c * FFN_COLS
        gate = conv(_dot(h2, wu_ref[:, c0:c0 + FFN_COLS]), c0)
        val = conv(_dot(h2, wu_ref[:, D_FF + c0:D_FF + c0 + FFN_COLS]), D_FF + c0)
        act = (_silu(gate) * val).astype(BF16)
        acc = acc + _dot(act, wd_ref[c0:c0 + FFN_COLS, :])
    o_ref[0] = x1 + gate2 * _rms(acc, gqf_ref[...])


def _tail(xa, ctx, yf, yb, xbc, z, y5, mod, layer, p, ctx_len, latent_only):
    bsz = xa.shape[0]
    tm = TOKEN_TILE
    ctx_tiles = ctx_len // tm
    t_all = yf.shape[1]
    off = ctx_tiles if latent_only else 0
    nt = t_all // tm - off
    n_chunks = tm // S5_Q
    assert not (latent_only and ctx is not None)

    def mod_map(b, t):
        return (layer * MOD_ROWS + jnp.where(t + off < ctx_tiles, bsz, b), 0, 0)

    tok = lambda w: pl.BlockSpec((1, tm, w), lambda b, t: (b, t + off, 0))
    if ctx is None:
        x_specs, x_args = [tok(D_MODEL)], [xa]
    else:
        x_specs, x_args, _ = _token_specs(xa, ctx, ctx_tiles, tm)
    return pl.pallas_call(
        functools.partial(_tail_kernel, ctx_tiles=ctx_tiles, tile_offset=off, split=ctx is not None),
        grid=(bsz, nt),
        in_specs=x_specs + [
            tok(D_SSD), tok(D_SSD), tok(D_SSD), tok(D_SSD),
            pl.BlockSpec((n_chunks, S5_Q, D_S5), lambda b, t: (t + off, b, 0)),
            pl.BlockSpec((1, 1, 6 * D_MODEL), mod_map),
            _full((1, D_SSD)), _full((1, D_SSD)),
            _resident((D_S5, D_S5)), _full((1, D_S5)),
            _resident((D_MODEL, D_MODEL)),
            _full((1, D_MODEL)), _full((1, D_MODEL)),
            _resident((D_MODEL, 2 * D_FF)),
            _full((3, 2 * D_FF)), _full((1, 2 * D_FF)),
            _resident((D_FF, D_MODEL)),
            _full((1, D_MODEL)),
        ],
        out_specs=pl.BlockSpec((1, tm, D_MODEL), lambda b, t: (b, t, 0)),
        out_shape=jax.ShapeDtypeStruct((bsz, nt * tm, D_MODEL), F32),
        compiler_params=pltpu.CompilerParams(vmem_limit_bytes=VMEM_LIMIT),
        name="tail",
    )(*x_args, yf, yb, xbc, z, y5, mod, p['ssd_d'], p['ssd_norm_g'], p['glu_w'], p['glu_b'], p['w_out'],
      p['g_post_mix'], p['g_pre_ffn'], p['w_up'], p['ffn_conv_w'], p['ffn_conv_b'], p['w_down'],
      p['g_post_ffn'])


def _row(v, width=None):
    v = v.astype(F32).reshape(1, -1)
    if width is not None and v.shape[1] < width:
        v = jnp.pad(v, ((0, 0), (0, width - v.shape[1])))
    return v


def kernel(x, c, ctx, c_ctx, w_ada, b_ada, g_pre_mix, g_post_mix, g_pre_ffn, g_post_ffn, w_in, ssd_conv_w,
           ssd_conv_b, ssd_dt_bias, ssd_a_log, ssd_d, ssd_norm_g, s5_a_re, s5_a_im, s5_log_dt, s5_b_re,
           s5_b_im, s5_c_re, s5_c_im, s5_d, s5_glu_w, s5_glu_b, w_out, ffn_w_up, ffn_conv_w, ffn_conv_b,
           ffn_w_down):
    bsz, seq, _ = x.shape
    ctx_len = ctx.shape[1]
    depth = w_in.shape[0]
    assert ctx_len == TOKEN_TILE and seq % TOKEN_TILE == 0 and bsz + 1 <= MOD_ROWS and bsz * 2 == 8

    cvec = jnp.concatenate([c, c_ctx[None, :], jnp.zeros((MOD_ROWS - bsz - 1, D_MODEL), F32)], axis=0)
    mod = _ada_table(cvec, w_ada, b_ada).reshape(depth * MOD_ROWS, 1, 6 * D_MODEL)
    xa, ca = x, ctx

    z0, z1, z2 = D_SSD, D_SSD + XBC_DIM, D_SSD + XBC_DIM + 2 * SSD_HEADS
    for l in range(depth):
        last = l == depth - 1
        wl = w_in[l]
        w_in_l = jnp.concatenate([wl[:, :z0], wl[:, z0:z1], wl[:, z2:], wl[:, z1:z2],
                                  jnp.zeros((D_MODEL, LANES - 2 * SSD_HEADS), F32)], axis=1).astype(BF16)
        z, xbc, dt, u = _inproj(xa, ca, mod, l, _row(g_pre_mix[l]), w_in_l, ssd_conv_w[l].astype(F32),
                                _row(ssd_conv_b[l]), _row(ssd_dt_bias[l], LANES), ctx_len)
        yf, yb = _ssd(xbc, dt, _row(ssd_a_log[l], LANES), ctx_len)
        tt, ws, wo, a16, dvec = _s5_weights(s5_a_re[l], s5_a_im[l], s5_log_dt[l], s5_b_re[l], s5_b_im[l],
                                            s5_c_re[l], s5_c_im[l], s5_d[l])
        y5 = _s5(u.reshape(-1, D_S5), tt, ws, wo, a16, dvec, bsz, ctx_len).reshape(u.shape)
        p = {
            'ssd_d': _row(jnp.repeat(ssd_d[l], SSD_HEAD_DIM)), 'ssd_norm_g': _row(ssd_norm_g[l]),
            'glu_w': s5_glu_w[l].astype(BF16), 'glu_b': _row(s5_glu_b[l]), 'w_out': w_out[l].astype(BF16),
            'g_post_mix': _row(g_post_mix[l]), 'g_pre_ffn': _row(g_pre_ffn[l]),
            'w_up': ffn_w_up[l].astype(BF16), 'ffn_conv_w': ffn_conv_w[l].astype(F32),
            'ffn_conv_b': _row(ffn_conv_b[l]), 'w_down': ffn_w_down[l].astype(BF16),
            'g_post_ffn': _row(g_post_ffn[l]),
        }
        xa = _tail(xa, ca, yf, yb, xbc, z, y5, mod, l, p, ctx_len, latent_only=last)
        ca = None
    return xa
```

```python
import functools

import jax
import jax.numpy as jnp
from jax import lax
from jax.experimental import pallas as pl
from jax.experimental.pallas import tpu as pltpu

F32 = jnp.float32
BF16 = jnp.bfloat16

D_MODEL = 1024
D_SSD = 512
SSD_HEADS = 8
SSD_HEAD_DIM = 64
SSD_GROUPS = 2
SSD_STATE = 128
SSD_CHUNK = 128
XBC_DIM = D_SSD + 2 * SSD_GROUPS * SSD_STATE
D_S5 = 512
S5_CH = 16
S5_GROUPS = 32
S5_STATE = 64
S5_Q = 16
S5_PAIRS = S5_GROUPS // 2
D_FF = 2816
FFN_COLS = 256
GRID_W = 64
NORM_EPS = 1e-6
LANES = 128
LATENT_TILE = 512
W_IN_COLS = D_SSD + XBC_DIM + D_S5 + LANES
MOD_ROWS = 8
VMEM_LIMIT = 56 * 1024 * 1024


def _silu(x):
    return x * (1.0 / (1.0 + jnp.exp(-x)))


def _sigmoid(x):
    return 1.0 / (1.0 + jnp.exp(-x))


def _gelu_tanh(x):
    c = 0.7978845608028654
    return x * (0.5 * (1.0 + jnp.tanh(c * (x + 0.044715 * (x * x * x)))))


def _softplus(x):
    return jnp.maximum(x, 0.0) + jnp.log(1.0 + jnp.exp(-jnp.abs(x)))


def _rms(x, g):
    ms = jnp.mean(x * x, axis=-1, keepdims=True)
    return x * lax.rsqrt(ms + NORM_EPS) * g


def _dot(a, b):
    return jnp.dot(a, b, preferred_element_type=F32)


def _full(shape):
    n = len(shape)
    return pl.BlockSpec(shape, lambda *_: (0,) * n)


def _resident(shape):
    n = len(shape)
    return pl.BlockSpec(shape, lambda *_: (0,) * n, pipeline_mode=pl.Buffered(1))


def _ada_kernel(c_ref, w_ref, b_ref, o_ref):
    c = _silu(c_ref[...]).astype(BF16)
    o_ref[0] = _dot(c, w_ref[0].astype(BF16)) + b_ref[0]


def _ada_table(cvec, w_ada, b_ada):
    depth = w_ada.shape[0]
    n_col = 6 * D_MODEL // D_MODEL
    return pl.pallas_call(
        _ada_kernel,
        grid=(depth, n_col),
        in_specs=[
            _full((MOD_ROWS, D_MODEL)),
            pl.BlockSpec((1, D_MODEL, D_MODEL), lambda l, n: (l, 0, n)),
            pl.BlockSpec((1, 1, D_MODEL), lambda l, n: (l, 0, n)),
        ],
        out_specs=pl.BlockSpec((1, MOD_ROWS, D_MODEL), lambda l, n: (l, 0, n)),
        out_shape=jax.ShapeDtypeStruct((depth, MOD_ROWS, 6 * D_MODEL), F32),
        name="ada_table",
    )(cvec, w_ada, b_ada.reshape(depth, 1, 6 * D_MODEL))


def _inproj_kernel(*refs, halo):
    if halo:
        (x_ref, xp_ref, xn_ref, mod_ref, g_ref, w_ref, cw_ref, cb_ref, dtb_ref,
         z_ref, xbc_ref, dt_ref, u_ref) = refs
    else:
        (x_ref, mod_ref, g_ref, w_ref, cw_ref, cb_ref, dtb_ref, _, _, _, _,
         z_ref, xbc_ref, dt_ref, u_ref) = refs
    t = pl.program_id(1)
    nt = pl.num_programs(1)
    tm = x_ref.shape[1]
    x_tile = x_ref[0]
    g = g_ref[...]
    mod = mod_ref[0]
    shift = mod[:, 0:D_MODEL]
    scale = mod[:, D_MODEL:2 * D_MODEL]

    def prenorm(xv):
        return (_rms(xv, g) * (1.0 + scale) + shift).astype(BF16)

    proj = _dot(prenorm(x_tile), w_ref[...])
    z_ref[0] = proj[:, 0:D_SSD]
    u_ref[...] = proj[:, D_SSD + XBC_DIM:D_SSD + XBC_DIM + D_S5].reshape(tm // S5_Q, S5_Q, D_S5)
    dt_ref[0] = _softplus(proj[:, D_SSD + XBC_DIM + D_S5:] + dtb_ref[...])
    xbc = proj[:, D_SSD:D_SSD + XBC_DIM]

    if halo:
        w_xbc = w_ref[:, D_SSD:D_SSD + XBC_DIM]
        edge_prev = _dot(prenorm(xp_ref[0]), w_xbc)[7:8, :] * (t != 0).astype(F32)
        edge_next = _dot(prenorm(xn_ref[0]), w_xbc)[0:1, :] * (t != nt - 1).astype(F32)
        edge_prev = jnp.broadcast_to(edge_prev, (8, XBC_DIM))
        edge_next = jnp.broadcast_to(edge_next, (8, XBC_DIM))
    else:
        edge_prev = edge_next = jnp.zeros((8, XBC_DIM), F32)
    r8 = lax.broadcasted_iota(jnp.int32, (8, XBC_DIM), 0)
    left = pltpu.roll(xbc, 1, 0)
    left = jnp.concatenate([jnp.where(r8 == 0, edge_prev, left[0:8]), left[8:]], axis=0)
    right = pltpu.roll(xbc, tm - 1, 0)
    right = jnp.concatenate([right[:tm - 8], jnp.where(r8 == 7, edge_next, right[tm - 8:])], axis=0)
    conv = left * cw_ref[0:1, :] + xbc * cw_ref[1:2, :] + right * cw_ref[2:3, :] + cb_ref[...]
    xbc_ref[0] = _silu(conv)


def _inproj(xa, ca, ctx_blk, mod, layer, g_pre, w_in, conv_w, conv_b, dt_bias, seq, ctx_len):
    bsz = xa.shape[0]
    t_all = seq + ctx_len
    tm = LATENT_TILE
    rows8 = tm // 8
    last8 = seq // 8 - 1
    weights = [_full((1, D_MODEL)), _resident((D_MODEL, W_IN_COLS)), _full((3, XBC_DIM)),
               _full((1, XBC_DIM)), _full((1, LANES))]
    w_args = (g_pre, w_in, conv_w, conv_b, dt_bias)
    out_shape = [
        jax.ShapeDtypeStruct((bsz, t_all, D_SSD), F32),
        jax.ShapeDtypeStruct((bsz, t_all, XBC_DIM), F32),
        jax.ShapeDtypeStruct((bsz, t_all, LANES), F32),
        jax.ShapeDtypeStruct((t_all // S5_Q, bsz * S5_Q, D_S5), F32),
    ]
    params = pltpu.CompilerParams(vmem_limit_bytes=VMEM_LIMIT)

    tok = lambda w: pl.BlockSpec((1, tm, w), lambda b, t: (b, t, 0))
    outs = pl.pallas_call(
        functools.partial(_inproj_kernel, halo=True),
        grid=(bsz, seq // tm),
        in_specs=[
            tok(D_MODEL),
            pl.BlockSpec((1, 8, D_MODEL), lambda b, t: (b, jnp.maximum(t * rows8 - 1, 0), 0)),
            pl.BlockSpec((1, 8, D_MODEL), lambda b, t: (b, jnp.minimum((t + 1) * rows8, last8), 0)),
            pl.BlockSpec((1, 1, 6 * D_MODEL), lambda b, t: (layer * MOD_ROWS + b, 0, 0)),
        ] + weights,
        out_specs=[tok(D_SSD), tok(XBC_DIM), tok(LANES),
                   pl.BlockSpec((tm // S5_Q, S5_Q, D_S5), lambda b, t: (t, b, 0))],
        out_shape=out_shape,
        compiler_params=params,
        name="inproj",
    )(xa, xa, xa, mod, *w_args)

    tc = ctx_len
    c0 = seq // tc
    ctok = lambda w: pl.BlockSpec((1, tc, w), lambda b, t: (b, c0, 0))
    anyspec = pl.BlockSpec(memory_space=pl.ANY)
    return pl.pallas_call(
        functools.partial(_inproj_kernel, halo=False),
        grid=(bsz, 1),
        in_specs=[
            pl.BlockSpec((1, tc, D_MODEL), lambda b, t: (b, ctx_blk, 0)),
            pl.BlockSpec((1, 1, 6 * D_MODEL), lambda b, t: (layer * MOD_ROWS + bsz, 0, 0)),
        ] + weights + [anyspec] * 4,
        out_specs=[ctok(D_SSD), ctok(XBC_DIM), ctok(LANES),
                   pl.BlockSpec((tc // S5_Q, S5_Q, D_S5), lambda b, t: (c0, b, 0))],
        out_shape=out_shape,
        input_output_aliases={7: 0, 8: 1, 9: 2, 10: 3},
        compiler_params=params,
        name="inproj_ctx",
    )(ca, mod, *w_args, *outs)


def _split3(x):
    hi = x.astype(BF16)
    r1 = x - hi.astype(F32)
    mid = r1.astype(BF16)
    lo = (r1 - mid.astype(F32)).astype(BF16)
    return hi, mid, lo


def _ssd_direction(d, x_ref, dt_ref, y_ref, st_ref, a_row):
    q = SSD_CHUNK
    fwd = d == 0
    ri = lax.broadcasted_iota(jnp.int32, (q, q), 0)
    ci = lax.broadcasted_iota(jnp.int32, (q, q), 1)
    mask = (ri >= ci) if fwd else (ri <= ci)
    tri = mask.astype(BF16)
    lo64 = lax.broadcasted_iota(jnp.int32, (1, LANES), 1) < SSD_HEAD_DIM

    xbc = x_ref[0]
    dt = dt_ref[0]
    hi, mid, lo = _split3(dt * a_row)
    cum = _dot(tri, hi) + _dot(tri, mid) + _dot(tri, lo)
    tot = cum[q - 1:q, :] if fwd else cum[0:1, :]
    wcol = dt * jnp.exp(tot - cum)
    etot = jnp.exp(tot)
    cum_t = cum.T
    dt_t = dt.T
    xs = xbc[:, 0:D_SSD]
    xs_b = xs.astype(BF16)

    new_states = []
    for g in range(SSD_GROUPS):
        bm = xbc[:, D_SSD + g * SSD_STATE:D_SSD + (g + 1) * SSD_STATE]
        cm = xbc[:, D_SSD + (SSD_GROUPS + g) * SSD_STATE:D_SSD + (SSD_GROUPS + g + 1) * SSD_STATE]
        cb = lax.dot_general(cm.astype(BF16), bm.astype(BF16), (((1,), (1,)), ((), ())),
                             preferred_element_type=F32)
        bm_t = bm.T.astype(BF16)
        xw_parts = []
        for pair in (2 * g, 2 * g + 1):
            sl = slice(pair * LANES, (pair + 1) * LANES)
            xs_pair = xs_b[:, sl]
            st_pair = st_ref[d, :, sl].astype(BF16)
            acc = None
            wl = []
            for hh in range(2):
                lane = SSD_HEADS * d + 2 * pair + hh
                cl = jnp.broadcast_to(cum[:, lane:lane + 1], (q, q))
                seg = cl - cum_t[lane:lane + 1, :]
                decay = jnp.exp(jnp.where(mask, seg, -jnp.inf))
                m = cb * decay * dt_t[lane:lane + 1, :]
                cme = cm * jnp.exp(cl)
                lhs = jnp.concatenate([m, cme], axis=1).astype(BF16)
                sel = lo64 if hh == 0 else jnp.logical_not(lo64)
                rhs = jnp.concatenate([jnp.where(sel, xs_pair, jnp.zeros_like(xs_pair)),
                                       jnp.where(sel, st_pair, jnp.zeros_like(st_pair))], axis=0)
                part = _dot(lhs, rhs)
                acc = part if acc is None else acc + part
                wl.append(jnp.broadcast_to(wcol[:, lane:lane + 1], (q, LANES)))
            y_ref[0, :, sl] = acc
            xw_parts.append((xs[:, sl] * jnp.where(lo64, wl[0], wl[1])).astype(BF16))
        new_states.append(_dot(bm_t, jnp.concatenate(xw_parts, axis=1)))

    for pair in range(SSD_HEADS // 2):
        sl = slice(pair * LANES, (pair + 1) * LANES)
        l0 = SSD_HEADS * d + 2 * pair
        dpair = jnp.where(lo64, jnp.broadcast_to(etot[:, l0:l0 + 1], (1, LANES)),
                          jnp.broadcast_to(etot[:, l0 + 1:l0 + 2], (1, LANES)))
        new = new_states[pair // 2][:, (pair % 2) * LANES:(pair % 2 + 1) * LANES]
        st_ref[d, :, sl] = st_ref[d, :, sl] * dpair + new


def _ssd_kernel(xf_ref, dtf_ref, xb_ref, dtb_ref, alog_ref, yf_ref, yb_ref, st_ref):
    @pl.when(pl.program_id(1) == 0)
    def _():
        st_ref[...] = jnp.zeros_like(st_ref)

    a_row = -jnp.exp(alog_ref[...])
    _ssd_direction(0, xf_ref, dtf_ref, yf_ref, st_ref, a_row)
    _ssd_direction(1, xb_ref, dtb_ref, yb_ref, st_ref, a_row)


def _ssd(xbc, dt, a_log_row, ctx_len):
    bsz, t_all, _ = xbc.shape
    q = SSD_CHUNK
    nch = t_all // q
    ncc = ctx_len // q

    nlc = nch - ncc

    def fmap(b, i):
        return (b, jnp.where(i < ncc, nlc + i, i - ncc), 0)

    def bmap(b, i):
        return (b, nch - 1 - i, 0)

    return pl.pallas_call(
        _ssd_kernel,
        grid=(bsz, nch),
        in_specs=[
            pl.BlockSpec((1, q, XBC_DIM), fmap),
            pl.BlockSpec((1, q, LANES), fmap),
            pl.BlockSpec((1, q, XBC_DIM), bmap),
            pl.BlockSpec((1, q, LANES), bmap),
            _full((1, LANES)),
        ],
        out_specs=[pl.BlockSpec((1, q, D_SSD), fmap), pl.BlockSpec((1, q, D_SSD), bmap)],
        out_shape=[jax.ShapeDtypeStruct((bsz, t_all, D_SSD), F32)] * 2,
        scratch_shapes=[pltpu.VMEM((2, SSD_STATE, D_SSD), F32)],
        compiler_params=pltpu.CompilerParams(dimension_semantics=("arbitrary", "arbitrary"),
                                             vmem_limit_bytes=VMEM_LIMIT),
        name="ssd",
    )(xbc, dt, xbc, dt, a_log_row)


def _block_transpose8(xs):
    lane = lax.broadcasted_iota(jnp.int32, (1, LANES), 1)
    xs = list(xs)
    for d in (4, 2, 1):
        keep = jnp.bitwise_and(lane, S5_CH * d) == 0
        nxt = list(xs)
        for a in range(8):
            if a & d:
                continue
            nxt[a] = jnp.where(keep, xs[a], pltpu.roll(xs[a + d], S5_CH * d, 1))
            nxt[a + d] = jnp.where(keep, pltpu.roll(xs[a], LANES - S5_CH * d, 1), xs[a + d])
        xs = nxt
    return xs


def _s5_kernel(u_ref, tt_ref, ws_ref, wo_ref, a_ref, d_ref, y_ref, s_ref, ub_ref,
               *, rows_blk, n_tiles, ctx_tiles):
    phase = pl.program_id(1)
    rb = pl.program_id(2)
    q = S5_Q
    gw = q * S5_CH
    pw = 2 * gw
    n_pairs = u_ref.shape[1] // (2 * S5_CH)
    r0 = pl.multiple_of(rb * rows_blk, 8)

    def token_rows(ref, j):
        return ref[pl.ds(j, rows_blk, stride=q), :]

    @pl.when(phase == 0)
    def _():
        uj = [token_rows(u_ref, j) for j in range(q)]
        ug = [_block_transpose8(uj[0:8]), _block_transpose8(uj[8:16])]
        for p in range(n_pairs):
            up = jnp.concatenate([ug[0][2 * p], ug[1][2 * p], ug[0][2 * p + 1], ug[1][2 * p + 1]],
                                 axis=1).astype(BF16)
            ub_ref[pl.ds(r0, rows_blk), p * pw:(p + 1) * pw] = up
            s_ref[pl.ds(r0, rows_blk), p * pw:(p + 1) * pw] = _dot(up, ws_ref[0, p])

    lo = lax.broadcasted_iota(jnp.int32, (8, LANES), 0) < 4
    hi = jnp.logical_not(lo)

    def roll4(v):
        return pltpu.roll(v, 4, 0)

    def tile_step(kt, cre, cim, ar, ai, col, first):
        r0 = pl.multiple_of(kt * 8, 8)
        sre = s_ref[pl.ds(r0, 8), col:col + LANES]
        sim = s_ref[pl.ds(r0, 8), col + LANES:col + 2 * LANES]
        h1re = roll4(ar * cre - ai * cim + sre)
        h1im = roll4(ar * cim + ai * cre + sim)
        s_ref[pl.ds(r0, 8), col:col + LANES] = jnp.where(first, cre, h1re)
        s_ref[pl.ds(r0, 8), col + LANES:col + 2 * LANES] = jnp.where(first, cim, h1im)
        t2re = ar * h1re - ai * h1im + sre
        t2im = ar * h1im + ai * h1re + sim
        second = jnp.logical_not(first)
        return jnp.where(second, t2re, roll4(t2re)), jnp.where(second, t2im, roll4(t2im))

    @pl.when(jnp.logical_and(phase == 1, rb == 0))
    def _():
        def arow(p, r):
            return jnp.broadcast_to(a_ref[0, p, r:r + 1, :], (8, LANES))

        coef = [[arow(p, r) for r in range(4)] for p in range(n_pairs)]

        def body(i, carry):
            kf = jnp.where(i < ctx_tiles, n_tiles - ctx_tiles + i, i - ctx_tiles)
            kb = n_tiles - 1 - i
            out = []
            for p in range(n_pairs):
                fre, fim, bre, bim = carry[4 * p:4 * p + 4]
                arf, aif, arb, aib = coef[p]
                fre, fim = tile_step(kf, fre, fim, arf, aif, p * pw, lo)
                bre, bim = tile_step(kb, bre, bim, arb, aib, p * pw + 2 * LANES, hi)
                out += [fre, fim, bre, bim]
            return tuple(out)

        zero = jnp.zeros((8, LANES), F32)
        lax.fori_loop(0, n_tiles, body, (zero,) * (4 * n_pairs))

    @pl.when(phase == 1)
    def _():
        yg = [[None] * 8, [None] * 8]
        for p in range(n_pairs):
            up = ub_ref[pl.ds(r0, rows_blk), p * pw:(p + 1) * pw]
            h = s_ref[pl.ds(r0, rows_blk), p * pw:(p + 1) * pw].astype(BF16)
            off = _dot(h, wo_ref[0, p])
            for gi in range(2):
                g = 2 * p + gi
                yp = _dot(up[:, gi * gw:(gi + 1) * gw], tt_ref[0, g]) + off[:, gi * gw:(gi + 1) * gw]
                yg[0][g] = yp[:, 0:LANES]
                yg[1][g] = yp[:, LANES:2 * LANES]
        for half in range(2):
            yi = _block_transpose8(yg[half])
            for il in range(8):
                i = 8 * half + il
                y_ref[pl.ds(i, rows_blk, stride=q), :] = yi[il] + token_rows(u_ref, i) * d_ref[0]


def _s5(u2d, tt, ws, wo, a16, dvec, layer, bsz, ctx_len):
    n_rows, width = u2d.shape
    rows = n_rows // S5_Q
    n_sg = width // LANES
    w0 = layer * n_sg
    n_blk = 4
    rows_blk = rows // n_blk
    assert rows_blk % 8 == 0
    n_tiles = rows // 8
    ctx_tiles = (ctx_len // S5_Q) * bsz // 8
    pairs_sg = S5_PAIRS // n_sg
    pw = 2 * S5_Q * S5_CH
    return pl.pallas_call(
        functools.partial(_s5_kernel, rows_blk=rows_blk, n_tiles=n_tiles, ctx_tiles=ctx_tiles),
        grid=(n_sg, 2, n_blk),
        in_specs=[
            pl.BlockSpec((rows_blk * S5_Q, LANES), lambda s, ph, rb: (rb, s)),
            pl.BlockSpec((1, 2 * pairs_sg, pw // 2, pw // 2), lambda s, ph, rb: (w0 + s, 0, 0, 0)),
            pl.BlockSpec((1, pairs_sg, pw, pw), lambda s, ph, rb: (w0 + s, 0, 0, 0)),
            pl.BlockSpec((1, pairs_sg, pw, pw), lambda s, ph, rb: (w0 + s, 0, 0, 0)),
            pl.BlockSpec((1, pairs_sg, 8, LANES), lambda s, ph, rb: (w0 + s, 0, 0, 0)),
            pl.BlockSpec((1, 1, LANES), lambda s, ph, rb: (w0 + s, 0, 0)),
        ],
        out_specs=pl.BlockSpec((rows_blk * S5_Q, LANES), lambda s, ph, rb: (ph * rb, s)),
        out_shape=jax.ShapeDtypeStruct(u2d.shape, F32),
        scratch_shapes=[pltpu.VMEM((rows, pairs_sg * pw), F32), pltpu.VMEM((rows, pairs_sg * pw), BF16)],
        compiler_params=pltpu.CompilerParams(dimension_semantics=("arbitrary", "arbitrary", "arbitrary"),
                                             vmem_limit_bytes=VMEM_LIMIT),
        name="s5",
    )(u2d, tt, ws, wo, a16, dvec)


def _cmul(ar, ai, br, bi):
    return ar * br - ai * bi, ar * bi + ai * br


def _cpow(br, bi, expo, nbits):
    rr = jnp.ones_like(br)
    ri = jnp.zeros_like(br)
    for bit in range(nbits):
        take = jnp.bitwise_and(lax.shift_right_logical(expo, bit), 1) == 1
        nr, ni = _cmul(rr, ri, br, bi)
        rr = jnp.where(take, nr, rr)
        ri = jnp.where(take, ni, ri)
        if bit + 1 < nbits:
            br, bi = _cmul(br, bi, br, bi)
    return rr, ri


def _s5_prep_kernel(rows_ref, lcol_ref, bt_ref, ct_ref, tt_ref, ws_ref, wo_ref, a_ref):
    q, ch, p = S5_Q, S5_CH, S5_STATE
    gw, pw = q * ch, 2 * q * ch
    lam_row = [(rows_ref[0, 2 * d:2 * d + 1, :], rows_ref[0, 2 * d + 1:2 * d + 2, :]) for d in range(2)]
    f_row = [(rows_ref[0, 4 + 2 * d:5 + 2 * d, :], rows_ref[0, 5 + 2 * d:6 + 2 * d, :]) for d in range(2)]
    bt_r, bt_i = bt_ref[0, 0], bt_ref[0, 1]
    ct_r, ct_i = ct_ref[0, 0], ct_ref[0, 1]

    ri = lax.broadcasted_iota(jnp.int32, (pw, 2 * p), 0)
    li = lax.broadcasted_iota(jnp.int32, (pw, 2 * p), 1)
    same = lax.shift_right_logical(ri, 8) == lax.shift_right_logical(li, 6)
    j = jnp.bitwise_and(lax.shift_right_logical(ri, 4), q - 1)
    b2_r = jnp.where(same, jnp.concatenate([bt_r] * (pw // ch), axis=0), 0.0)
    b2_i = jnp.where(same, jnp.concatenate([bt_i] * (pw // ch), axis=0), 0.0)
    blocks = []
    for d in range(2):
        expo = (q - 1 - j) if d == 0 else j
        pr, pi = _cpow(jnp.broadcast_to(lam_row[d][0], (pw, 2 * p)), jnp.broadcast_to(lam_row[d][1], (pw, 2 * p)),
                       expo, 4)
        gr, gi = _cmul(pr, pi, f_row[d][0], f_row[d][1])
        wr, wi = _cmul(b2_r, b2_i, gr, gi)
        blocks += [wr, wi]
    ws_ref[0] = jnp.concatenate(blocks, axis=1).astype(BF16)

    ri = lax.broadcasted_iota(jnp.int32, (2 * p, pw), 0)
    li = lax.broadcasted_iota(jnp.int32, (2 * p, pw), 1)
    row_g = lax.shift_right_logical(ri, 6)
    slot = lax.shift_right_logical(li, 4)
    same = row_g == lax.shift_right_logical(li, 8)
    i = jnp.bitwise_and(slot, q - 1)
    lam_col = [(jnp.concatenate([lcol_ref[0, 2 * d]] * (pw // LANES), axis=1),
                jnp.concatenate([lcol_ref[0, 2 * d + 1]] * (pw // LANES), axis=1)) for d in range(2)]
    blocks = []
    for d in range(2):
        expo = (i + 1) if d == 0 else (q - i)
        pr, pi = _cpow(lam_col[d][0], lam_col[d][1], expo, 5)
        cr, ci = _cmul(ct_r, ct_i, pr, pi)
        blocks += [jnp.where(same, cr, 0.0), jnp.where(same, -ci, 0.0)]
    wo_ref[0] = jnp.concatenate(blocks, axis=0).astype(BF16)

    fr, fi = _cpow(lam_col[0][0], lam_col[0][1], jnp.maximum(slot - (q - 1), 0), 4)
    br, bi = _cpow(lam_col[1][0], lam_col[1][1], jnp.maximum(q - 1 - slot, 0), 4)
    mf = _cmul(ct_r, ct_i, fr, fi)
    mb = _cmul(ct_r, ct_i, br, bi)
    use_f = jnp.logical_and(slot >= q - 1, slot <= 2 * q - 2)
    use_b = slot <= q - 1
    bb = [_cmul(bt_r, bt_i, f_row[d][0], f_row[d][1]) for d in range(2)]
    lhs = jnp.concatenate([bb[0][0], bb[0][1], bb[1][0], bb[1][1]], axis=1)
    for g in range(2):
        mine = row_g == g
        rhs = jnp.concatenate([jnp.where(jnp.logical_and(mine, use_f), mf[0], 0.0),
                               jnp.where(jnp.logical_and(mine, use_f), -mf[1], 0.0),
                               jnp.where(jnp.logical_and(mine, use_b), mb[0], 0.0),
                               jnp.where(jnp.logical_and(mine, use_b), -mb[1], 0.0)], axis=0)
        z = jnp.dot(lhs, rhs, preferred_element_type=F32, precision=lax.Precision.HIGHEST)
        for jb in range(q):
            shift = (pw - ch * (q - 1 - jb)) % pw
            win = z if shift == 0 else pltpu.roll(z, shift, 1)
            tt_ref[0, g, jb * ch:(jb + 1) * ch, :] = win[:, 0:gw].astype(BF16)

    out = []
    for d in range(2):
        ar, ai = lam_row[d]
        for _ in range(4):
            ar, ai = _cmul(ar, ai, ar, ai)
        out += [ar, ai]
    a_ref[0] = jnp.concatenate(out + [jnp.zeros((4, 2 * p), F32)], axis=0)


def _s5_weights(a_re, a_im, log_dt, b_re, b_im, c_re, c_im):
    depth = a_re.shape[0]
    q, ch, p, npair = S5_Q, S5_CH, S5_STATE, S5_PAIRS
    lr, li = a_re.astype(F32), a_im.astype(F32)
    step = jnp.exp(log_dt.astype(F32))[..., None]
    mag = jnp.exp(lr * step)
    lbr, lbi = mag * jnp.cos(li * step), mag * jnp.sin(li * step)
    den = lr * lr + li * li
    fr, fi = ((lbr - 1.0) * lr + lbi * li) / den, (lbi * lr - (lbr - 1.0) * li) / den

    def pair_rows(v):
        return v.reshape(depth, 2, npair, 2 * p).transpose(0, 2, 1, 3).reshape(depth * npair, 2, 2 * p)

    vals = [pair_rows(v) for v in (lbr, lbi, fr, fi)]
    rows = jnp.stack([vals[0][:, 0], vals[1][:, 0], vals[0][:, 1], vals[1][:, 1],
                      vals[2][:, 0], vals[3][:, 0], vals[2][:, 1], vals[3][:, 1]], axis=1)
    lcol = jnp.broadcast_to(rows[:, 0:4, :, None], (depth * npair, 4, 2 * p, LANES))

    def b_t(b):
        return b.astype(F32).reshape(depth * npair, 2, p, ch).transpose(0, 3, 1, 2).reshape(depth * npair, ch, 2 * p)

    def c_t(c):
        ct = c.astype(F32).reshape(depth * npair, 2, ch, p).transpose(0, 1, 3, 2).reshape(depth * npair, 2 * p, ch)
        return jnp.tile(ct, (1, 1, 2 * q))

    bt = jnp.stack([b_t(b_re), b_t(b_im)], axis=1)
    ct = jnp.stack([c_t(c_re), c_t(c_im)], axis=1)
    n = depth * npair
    gw, pw = q * ch, 2 * q * ch
    return pl.pallas_call(
        _s5_prep_kernel,
        grid=(n,),
        in_specs=[
            pl.BlockSpec((1, 8, 2 * p), lambda i: (i, 0, 0)),
            pl.BlockSpec((1, 4, 2 * p, LANES), lambda i: (i, 0, 0, 0)),
            pl.BlockSpec((1, 2, ch, 2 * p), lambda i: (i, 0, 0, 0)),
            pl.BlockSpec((1, 2, 2 * p, pw), lambda i: (i, 0, 0, 0)),
        ],
        out_specs=[
            pl.BlockSpec((1, 2, gw, gw), lambda i: (i, 0, 0, 0)),
            pl.BlockSpec((1, pw, pw), lambda i: (i, 0, 0)),
            pl.BlockSpec((1, pw, pw), lambda i: (i, 0, 0)),
            pl.BlockSpec((1, 8, 2 * p), lambda i: (i, 0, 0)),
        ],
        out_shape=[
            jax.ShapeDtypeStruct((n, 2, gw, gw), BF16),
            jax.ShapeDtypeStruct((n, pw, pw), BF16),
            jax.ShapeDtypeStruct((n, pw, pw), BF16),
            jax.ShapeDtypeStruct((n, 8, 2 * p), F32),
        ],
        compiler_params=pltpu.CompilerParams(vmem_limit_bytes=VMEM_LIMIT),
        name="s5_prep",
    )(rows, lcol, bt, ct)


def _tail_kernel(*refs, period, aliased):
    (x_ref, yf_ref, yb_ref, xs_ref, z_ref, y5_ref, mod_ref, dv_ref, ng_ref, gw_ref, gb_ref,
     wo_ref, gpm_ref, gpf_ref, wu_ref, cw_ref, cb_ref, wd_ref, gqf_ref) = refs[:19]
    o_ref = refs[-1]
    assert len(refs) == 20 + int(aliased)
    tm = x_ref.shape[1]
    d = D_MODEL
    x_tile = x_ref[0]
    mod = mod_ref[0]
    gate1, shift2, scale2, gate2 = (mod[:, 2 * d:3 * d], mod[:, 3 * d:4 * d],
                                    mod[:, 4 * d:5 * d], mod[:, 5 * d:6 * d])

    y_ssd = (yf_ref[0] + yb_ref[0] + xs_ref[0] * dv_ref[...]) * _silu(z_ref[0])
    y_ssd = _rms(y_ssd, ng_ref[...])
    g5 = _gelu_tanh(y5_ref[...].reshape(tm, D_S5))
    y_s5 = g5 * _sigmoid(_dot(g5.astype(BF16), gw_ref[...]) + gb_ref[...])
    mix = _dot(y_ssd.astype(BF16), wo_ref[0:D_SSD, :]) + _dot(y_s5.astype(BF16), wo_ref[D_SSD:, :])
    x1 = x_tile + gate1 * _rms(mix, gpm_ref[...])

    h2 = (_rms(x1, gpf_ref[...]) * (1.0 + scale2) + shift2).astype(BF16)
    pos = jnp.bitwise_and(lax.broadcasted_iota(jnp.int32, (tm, FFN_COLS), 0), period)
    has_left = pos != 0
    has_right = pos != period

    def conv(v, col):
        left = jnp.where(has_left, pltpu.roll(v, 1, 0), 0.0)
        right = jnp.where(has_right, pltpu.roll(v, tm - 1, 0), 0.0)
        w = cw_ref[:, col:col + FFN_COLS]
        return left * w[0:1, :] + v * w[1:2, :] + right * w[2:3, :] + cb_ref[:, col:col + FFN_COLS]

    acc = jnp.zeros((tm, d), F32)
    for c in range(D_FF // FFN_COLS):
        c0 = c * FFN_COLS
        gate = conv(_dot(h2, wu_ref[:, c0:c0 + FFN_COLS]), c0)
        val = conv(_dot(h2, wu_ref[:, D_FF + c0:D_FF + c0 + FFN_COLS]), D_FF + c0)
        act = (_silu(gate) * val).astype(BF16)
        acc = acc + _dot(act, wd_ref[c0:c0 + FFN_COLS, :])
    o_ref[0] = x1 + gate2 * _rms(acc, gqf_ref[...])


def _tail(xa, ca, ctx_blk, yf, yb, xbc, z, y5, mod, layer, p, seq, ctx_len, with_context):
    bsz = xa.shape[0]
    t_out = seq + (ctx_len if with_context else 0)
    weights = [
        _full((1, D_SSD)), _full((1, D_SSD)),
        _resident((D_S5, D_S5)), _full((1, D_S5)),
        _resident((D_MODEL, D_MODEL)),
        _full((1, D_MODEL)), _full((1, D_MODEL)),
        _resident((D_MODEL, 2 * D_FF)),
        _full((3, 2 * D_FF)), _full((1, 2 * D_FF)),
        _resident((D_FF, D_MODEL)),
        _full((1, D_MODEL)),
    ]
    w_args = (p['ssd_d'], p['ssd_norm_g'], p['glu_w'], p['glu_b'], p['w_out'], p['g_post_mix'],
              p['g_pre_ffn'], p['w_up'], p['ffn_conv_w'], p['ffn_conv_b'], p['w_down'], p['g_post_ffn'])
    out_shape = jax.ShapeDtypeStruct((bsz, t_out, D_MODEL), F32)
    params = pltpu.CompilerParams(vmem_limit_bytes=VMEM_LIMIT)

    tm = LATENT_TILE
    tok = lambda w: pl.BlockSpec((1, tm, w), lambda b, t: (b, t, 0))
    out = pl.pallas_call(
        functools.partial(_tail_kernel, period=GRID_W - 1, aliased=False),
        grid=(bsz, seq // tm),
        in_specs=[
            tok(D_MODEL), tok(D_SSD), tok(D_SSD), tok(D_SSD), tok(D_SSD),
            pl.BlockSpec((tm // S5_Q, S5_Q, D_S5), lambda b, t: (t, b, 0)),
            pl.BlockSpec((1, 1, 6 * D_MODEL), lambda b, t: (layer * MOD_ROWS + b, 0, 0)),
        ] + weights,
        out_specs=tok(D_MODEL),
        out_shape=out_shape,
        compiler_params=params,
        name="tail",
    )(xa, yf, yb, xbc, z, y5, mod, *w_args)
    if not with_context:
        return out

    tc = ctx_len
    c0 = seq // tc
    ctok = lambda w: pl.BlockSpec((1, tc, w), lambda b, t: (b, c0, 0))
    return pl.pallas_call(
        functools.partial(_tail_kernel, period=tc - 1, aliased=True),
        grid=(bsz, 1),
        in_specs=[
            pl.BlockSpec((1, tc, D_MODEL), lambda b, t: (b, ctx_blk, 0)),
            ctok(D_SSD), ctok(D_SSD), ctok(D_SSD), ctok(D_SSD),
            pl.BlockSpec((tc // S5_Q, S5_Q, D_S5), lambda b, t: (c0, b, 0)),
            pl.BlockSpec((1, 1, 6 * D_MODEL), lambda b, t: (layer * MOD_ROWS + bsz, 0, 0)),
        ] + weights + [pl.BlockSpec(memory_space=pl.ANY)],
        out_specs=ctok(D_MODEL),
        out_shape=out_shape,
        input_output_aliases={19: 0},
        compiler_params=params,
        name="tail_ctx",
    )(ca, yf, yb, xbc, z, y5, mod, *w_args, out)


def _row(v, width=None):
    v = v.astype(F32).reshape(1, -1)
    if width is not None and v.shape[1] < width:
        v = jnp.pad(v, ((0, 0), (0, width - v.shape[1])))
    return v


def kernel(x, c, ctx, c_ctx, w_ada, b_ada, g_pre_mix, g_post_mix, g_pre_ffn, g_post_ffn, w_in, ssd_conv_w,
           ssd_conv_b, ssd_dt_bias, ssd_a_log, ssd_d, ssd_norm_g, s5_a_re, s5_a_im, s5_log_dt, s5_b_re,
           s5_b_im, s5_c_re, s5_c_im, s5_d, s5_glu_w, s5_glu_b, w_out, ffn_w_up, ffn_conv_w, ffn_conv_b,
           ffn_w_down):
    bsz, seq, _ = x.shape
    ctx_len = ctx.shape[1]
    depth = w_in.shape[0]
    assert seq % LATENT_TILE == 0 and seq % ctx_len == 0 and ctx_len % SSD_CHUNK == 0
    assert bsz + 1 <= MOD_ROWS and bsz * 2 == 8

    cvec = jnp.concatenate([c, c_ctx[None, :], jnp.zeros((MOD_ROWS - bsz - 1, D_MODEL), F32)], axis=0)
    mod = _ada_table(cvec, w_ada, b_ada).reshape(depth * MOD_ROWS, 1, 6 * D_MODEL)
    xa, ca, ctx_blk = x, ctx, 0

    n_sg = D_S5 // LANES
    psg = S5_PAIRS // n_sg
    gw = S5_Q * S5_CH
    tt, ws, wo, a16 = _s5_weights(s5_a_re, s5_a_im, s5_log_dt, s5_b_re, s5_b_im, s5_c_re, s5_c_im)
    tt = tt.reshape(depth * n_sg, 2 * psg, gw, gw)
    ws = ws.reshape(depth * n_sg, psg, 2 * gw, 2 * gw)
    wo = wo.reshape(depth * n_sg, psg, 2 * gw, 2 * gw)
    a16 = a16.reshape(depth * n_sg, psg, 8, LANES)
    dvec = s5_d.astype(F32).reshape(depth * n_sg, 1, LANES)

    z0, z1, z2 = D_SSD, D_SSD + XBC_DIM, D_SSD + XBC_DIM + 2 * SSD_HEADS
    for l in range(depth):
        last = l == depth - 1
        wl = w_in[l]
        w_in_l = jnp.concatenate([wl[:, :z0], wl[:, z0:z1], wl[:, z2:], wl[:, z1:z2],
                                  jnp.zeros((D_MODEL, LANES - 2 * SSD_HEADS), F32)], axis=1).astype(BF16)
        z, xbc, dt, u = _inproj(xa, ca, ctx_blk, mod, l, _row(g_pre_mix[l]), w_in_l, ssd_conv_w[l].astype(F32),
                                _row(ssd_conv_b[l]), _row(ssd_dt_bias[l], LANES), seq, ctx_len)
        yf, yb = _ssd(xbc, dt, _row(ssd_a_log[l], LANES), ctx_len)
        y5 = _s5(u.reshape(-1, D_S5), tt, ws, wo, a16, dvec, l, bsz, ctx_len).reshape(u.shape)
        p = {
            'ssd_d': _row(jnp.repeat(ssd_d[l], SSD_HEAD_DIM)), 'ssd_norm_g': _row(ssd_norm_g[l]),
            'glu_w': s5_glu_w[l].astype(BF16), 'glu_b': _row(s5_glu_b[l]), 'w_out': w_out[l].astype(BF16),
            'g_post_mix': _row(g_post_mix[l]), 'g_pre_ffn': _row(g_pre_ffn[l]),
            'w_up': ffn_w_up[l].astype(BF16), 'ffn_conv_w': ffn_conv_w[l].astype(F32),
            'ffn_conv_b': _row(ffn_conv_b[l]), 'w_down': ffn_w_down[l].astype(BF16),
            'g_post_ffn': _row(g_post_ffn[l]),
        }
        xa = _tail(xa, ca, ctx_blk, yf, yb, xbc, z, y5, mod, l, p, seq, ctx_len, with_context=not last)
        ca, ctx_blk = xa, seq // ctx_len
    return xa
```

```python
import functools

import jax
import jax.numpy as jnp
from jax import lax
from jax.experimental import pallas as pl
from jax.experimental.pallas import tpu as pltpu

F32 = jnp.float32
BF16 = jnp.bfloat16

D_MODEL = 1024
D_SSD = 512
SSD_HEADS = 8
SSD_HEAD_DIM = 64
SSD_GROUPS = 2
SSD_STATE = 128
SSD_CHUNK = 128
SSD_BLOCK = 256
XBC_DIM = D_SSD + 2 * SSD_GROUPS * SSD_STATE
D_S5 = 512
S5_CH = 16
S5_GROUPS = 32
S5_STATE = 64
S5_Q = 16
S5_PAIRS = S5_GROUPS // 2
D_FF = 2816
FFN_COLS = 256
GRID_W = 64
NORM_EPS = 1e-6
LANES = 128
LATENT_TILE = 512
W_IN_COLS = D_SSD + XBC_DIM + D_S5 + LANES
MOD_ROWS = 8
VMEM_LIMIT = 56 * 1024 * 1024


def _silu(x):
    return x * (1.0 / (1.0 + jnp.exp(-x)))


def _sigmoid(x):
    return 1.0 / (1.0 + jnp.exp(-x))


def _gelu_tanh(x):
    c = 0.7978845608028654
    return x * (0.5 * (1.0 + jnp.tanh(c * (x + 0.044715 * (x * x * x)))))


def _softplus(x):
    return jnp.maximum(x, 0.0) + jnp.log(1.0 + jnp.exp(-jnp.abs(x)))


def _rms(x, g):
    ms = jnp.mean(x * x, axis=-1, keepdims=True)
    return x * lax.rsqrt(ms + NORM_EPS) * g


def _dot(a, b):
    return jnp.dot(a, b, preferred_element_type=F32)


def _full(shape):
    n = len(shape)
    return pl.BlockSpec(shape, lambda *_: (0,) * n)


def _resident(shape):
    n = len(shape)
    return pl.BlockSpec(shape, lambda *_: (0,) * n, pipeline_mode=pl.Buffered(1))


def _ada_kernel(c_ref, w_ref, b_ref, o_ref):
    c = _silu(c_ref[...]).astype(BF16)
    o_ref[0] = _dot(c, w_ref[0].astype(BF16)) + b_ref[0]


def _ada_table(cvec, w_ada, b_ada):
    depth = w_ada.shape[0]
    n_col = 6 * D_MODEL // D_MODEL
    return pl.pallas_call(
        _ada_kernel,
        grid=(depth, n_col),
        in_specs=[
            _full((MOD_ROWS, D_MODEL)),
            pl.BlockSpec((1, D_MODEL, D_MODEL), lambda l, n: (l, 0, n)),
            pl.BlockSpec((1, 1, D_MODEL), lambda l, n: (l, 0, n)),
        ],
        out_specs=pl.BlockSpec((1, MOD_ROWS, D_MODEL), lambda l, n: (l, 0, n)),
        out_shape=jax.ShapeDtypeStruct((depth, MOD_ROWS, 6 * D_MODEL), F32),
        name="ada_table",
    )(cvec, w_ada, b_ada.reshape(depth, 1, 6 * D_MODEL))


def _inproj_kernel(*refs, halo):
    if halo:
        (x_ref, xp_ref, xn_ref, mod_ref, g_ref, w_ref, cw_ref, cb_ref, dtb_ref,
         z_ref, xbc_ref, dt_ref, u_ref) = refs
    else:
        (x_ref, mod_ref, g_ref, w_ref, cw_ref, cb_ref, dtb_ref, _, _, _, _,
         z_ref, xbc_ref, dt_ref, u_ref) = refs
    t = pl.program_id(1)
    nt = pl.num_programs(1)
    tm = x_ref.shape[1]
    x_tile = x_ref[0]
    g = g_ref[...]
    mod = mod_ref[0]
    shift = mod[:, 0:D_MODEL]
    scale = mod[:, D_MODEL:2 * D_MODEL]

    def prenorm(xv):
        return (_rms(xv, g) * (1.0 + scale) + shift).astype(BF16)

    proj = _dot(prenorm(x_tile), w_ref[...])
    z_ref[0] = proj[:, 0:D_SSD]
    u_ref[...] = proj[:, D_SSD + XBC_DIM:D_SSD + XBC_DIM + D_S5].reshape(tm // S5_Q, S5_Q, D_S5)
    dt_ref[0] = _softplus(proj[:, D_SSD + XBC_DIM + D_S5:] + dtb_ref[...])
    xbc = proj[:, D_SSD:D_SSD + XBC_DIM]

    if halo:
        w_xbc = w_ref[:, D_SSD:D_SSD + XBC_DIM]
        edge_prev = _dot(prenorm(xp_ref[0]), w_xbc)[7:8, :] * (t != 0).astype(F32)
        edge_next = _dot(prenorm(xn_ref[0]), w_xbc)[0:1, :] * (t != nt - 1).astype(F32)
        edge_prev = jnp.broadcast_to(edge_prev, (8, XBC_DIM))
        edge_next = jnp.broadcast_to(edge_next, (8, XBC_DIM))
    else:
        edge_prev = edge_next = jnp.zeros((8, XBC_DIM), F32)
    r8 = lax.broadcasted_iota(jnp.int32, (8, XBC_DIM), 0)
    left = pltpu.roll(xbc, 1, 0)
    left = jnp.concatenate([jnp.where(r8 == 0, edge_prev, left[0:8]), left[8:]], axis=0)
    right = pltpu.roll(xbc, tm - 1, 0)
    right = jnp.concatenate([right[:tm - 8], jnp.where(r8 == 7, edge_next, right[tm - 8:])], axis=0)
    conv = left * cw_ref[0:1, :] + xbc * cw_ref[1:2, :] + right * cw_ref[2:3, :] + cb_ref[...]
    xbc_ref[0] = _silu(conv)


def _inproj(xa, ca, ctx_blk, mod, layer, g_pre, w_in, conv_w, conv_b, dt_bias, seq, ctx_len):
    bsz = xa.shape[0]
    t_all = seq + ctx_len
    tm = LATENT_TILE
    rows8 = tm // 8
    last8 = seq // 8 - 1
    weights = [_full((1, D_MODEL)), _resident((D_MODEL, W_IN_COLS)), _full((3, XBC_DIM)),
               _full((1, XBC_DIM)), _full((1, LANES))]
    w_args = (g_pre, w_in, conv_w, conv_b, dt_bias)
    out_shape = [
        jax.ShapeDtypeStruct((bsz, t_all, D_SSD), F32),
        jax.ShapeDtypeStruct((bsz, t_all, XBC_DIM), F32),
        jax.ShapeDtypeStruct((bsz, t_all, LANES), F32),
        jax.ShapeDtypeStruct((t_all // S5_Q, bsz * S5_Q, D_S5), F32),
    ]
    params = pltpu.CompilerParams(vmem_limit_bytes=VMEM_LIMIT)

    tok = lambda w: pl.BlockSpec((1, tm, w), lambda b, t: (b, t, 0))
    outs = pl.pallas_call(
        functools.partial(_inproj_kernel, halo=True),
        grid=(bsz, seq // tm),
        in_specs=[
            tok(D_MODEL),
            pl.BlockSpec((1, 8, D_MODEL), lambda b, t: (b, jnp.maximum(t * rows8 - 1, 0), 0)),
            pl.BlockSpec((1, 8, D_MODEL), lambda b, t: (b, jnp.minimum((t + 1) * rows8, last8), 0)),
            pl.BlockSpec((1, 1, 6 * D_MODEL), lambda b, t: (layer * MOD_ROWS + b, 0, 0)),
        ] + weights,
        out_specs=[tok(D_SSD), tok(XBC_DIM), tok(LANES),
                   pl.BlockSpec((tm // S5_Q, S5_Q, D_S5), lambda b, t: (t, b, 0))],
        out_shape=out_shape,
        compiler_params=params,
        name="inproj",
    )(xa, xa, xa, mod, *w_args)

    tc = ctx_len
    c0 = seq // tc
    ctok = lambda w: pl.BlockSpec((1, tc, w), lambda b, t: (b, c0, 0))
    anyspec = pl.BlockSpec(memory_space=pl.ANY)
    return pl.pallas_call(
        functools.partial(_inproj_kernel, halo=False),
        grid=(bsz, 1),
        in_specs=[
            pl.BlockSpec((1, tc, D_MODEL), lambda b, t: (b, ctx_blk, 0)),
            pl.BlockSpec((1, 1, 6 * D_MODEL), lambda b, t: (layer * MOD_ROWS + bsz, 0, 0)),
        ] + weights + [anyspec] * 4,
        out_specs=[ctok(D_SSD), ctok(XBC_DIM), ctok(LANES),
                   pl.BlockSpec((tc // S5_Q, S5_Q, D_S5), lambda b, t: (c0, b, 0))],
        out_shape=out_shape,
        input_output_aliases={7: 0, 8: 1, 9: 2, 10: 3},
        compiler_params=params,
        name="inproj_ctx",
    )(ca, mod, *w_args, *outs)


def _split3(x):
    hi = x.astype(BF16)
    r1 = x - hi.astype(F32)
    mid = r1.astype(BF16)
    lo = (r1 - mid.astype(F32)).astype(BF16)
    return hi, mid, lo


def _ssd_prologue(d, x_ref, dt_ref, a_row, r0):
    q = SSD_CHUNK
    fwd = d == 0
    ri = lax.broadcasted_iota(jnp.int32, (q, q), 0)
    ci = lax.broadcasted_iota(jnp.int32, (q, q), 1)
    mask = (ri >= ci) if fwd else (ri <= ci)
    end = q - 1 if fwd else 0
    xbc = x_ref[0, r0:r0 + q, :]
    dt = dt_ref[0, r0:r0 + q, :]
    tri = mask.astype(BF16)
    hi, mid, lo = _split3(dt * a_row)
    cum = _dot(tri, hi) + _dot(tri, mid) + _dot(tri, lo)
    bm, cm, cb = [], [], []
    for g in range(SSD_GROUPS):
        bm.append(xbc[:, D_SSD + g * SSD_STATE:D_SSD + (g + 1) * SSD_STATE])
        cm.append(xbc[:, D_SSD + (SSD_GROUPS + g) * SSD_STATE:D_SSD + (SSD_GROUPS + g + 1) * SSD_STATE])
        cb.append(lax.dot_general(cm[g].astype(BF16), bm[g].astype(BF16), (((1,), (1,)), ((), ())),
                                  preferred_element_type=F32))
    return dict(d=d, r0=r0, mask=mask, end=end, cum=cum, cum_t=cum.T, dt_t=dt.T,
                etot=jnp.exp(cum[end:end + 1, :]), xs_b=xbc[:, 0:D_SSD].astype(BF16),
                cm=cm, cb=cb, bm_t=[b.T for b in bm])


def _ssd_pair(c, pair, st_pair, y_ref):
    q = SSD_CHUNK
    d, r0, end, g = c['d'], c['r0'], c['end'], pair // 2
    lo64 = lax.broadcasted_iota(jnp.int32, (1, LANES), 1) < SSD_HEAD_DIM
    sl = slice(pair * LANES, (pair + 1) * LANES)
    xs_pair = c['xs_b'][:, sl]
    st_pair_b = st_pair.astype(BF16)
    acc = None
    new = None
    for hh in range(2):
        lane = SSD_HEADS * d + 2 * pair + hh
        crow = c['cum_t'][lane:lane + 1, :]
        drow = c['dt_t'][lane:lane + 1, :]
        cl = jnp.broadcast_to(c['cum'][:, lane:lane + 1], (q, q))
        decay = jnp.exp(jnp.where(c['mask'], cl - crow, -jnp.inf))
        m = c['cb'][g] * decay * drow
        cme = c['cm'][g] * jnp.exp(cl)
        lhs = jnp.concatenate([m, cme], axis=1).astype(BF16)
        sel = lo64 if hh == 0 else jnp.logical_not(lo64)
        xs_h = jnp.where(sel, xs_pair, jnp.zeros_like(xs_pair))
        rhs = jnp.concatenate([xs_h, jnp.where(sel, st_pair_b, jnp.zeros_like(st_pair_b))], axis=0)
        part = _dot(lhs, rhs)
        acc = part if acc is None else acc + part
        wrow = drow * jnp.exp(crow[:, end:end + 1] - crow)
        part = _dot((c['bm_t'][g] * wrow).astype(BF16), xs_h)
        new = part if new is None else new + part
    y_ref[0, r0:r0 + q, sl] = acc
    l0 = SSD_HEADS * d + 2 * pair
    etot = c['etot']
    dpair = jnp.where(lo64, jnp.broadcast_to(etot[:, l0:l0 + 1], (1, LANES)),
                      jnp.broadcast_to(etot[:, l0 + 1:l0 + 2], (1, LANES)))
    return st_pair * dpair + new


def _ssd_kernel(xf_ref, dtf_ref, xb_ref, dtb_ref, alog_ref, yf_ref, yb_ref, stf_ref, stb_ref):
    @pl.when(pl.program_id(1) == 0)
    def _():
        stf_ref[...] = jnp.zeros_like(stf_ref)
        stb_ref[...] = jnp.zeros_like(stb_ref)

    a_row = -jnp.exp(alog_ref[...])
    n_sub = xf_ref.shape[1] // SSD_CHUNK
    n_pair = SSD_HEADS // 2
    pf = [_ssd_prologue(0, xf_ref, dtf_ref, a_row, s * SSD_CHUNK) for s in range(n_sub)]
    pb = [_ssd_prologue(1, xb_ref, dtb_ref, a_row, (n_sub - 1 - s) * SSD_CHUNK) for s in range(n_sub)]
    sf = [stf_ref[:, p * LANES:(p + 1) * LANES] for p in range(n_pair)]
    sb = [stb_ref[:, p * LANES:(p + 1) * LANES] for p in range(n_pair)]
    for s in range(n_sub):
        for p in range(n_pair):
            sf[p] = _ssd_pair(pf[s], p, sf[p], yf_ref)
            sb[p] = _ssd_pair(pb[s], p, sb[p], yb_ref)
    for p in range(n_pair):
        stf_ref[:, p * LANES:(p + 1) * LANES] = sf[p]
        stb_ref[:, p * LANES:(p + 1) * LANES] = sb[p]


def _ssd(xbc, dt, a_log_row, ctx_len):
    bsz, t_all, _ = xbc.shape
    q = SSD_BLOCK
    assert ctx_len % q == 0 and t_all % q == 0
    nch = t_all // q
    ncc = ctx_len // q
    nlc = nch - ncc

    def fmap(b, i):
        return (b, jnp.where(i < ncc, nlc + i, i - ncc), 0)

    def bmap(b, i):
        return (b, nch - 1 - i, 0)

    return pl.pallas_call(
        _ssd_kernel,
        grid=(bsz, nch),
        in_specs=[
            pl.BlockSpec((1, q, XBC_DIM), fmap),
            pl.BlockSpec((1, q, LANES), fmap),
            pl.BlockSpec((1, q, XBC_DIM), bmap),
            pl.BlockSpec((1, q, LANES), bmap),
            _full((1, LANES)),
        ],
        out_specs=[pl.BlockSpec((1, q, D_SSD), fmap), pl.BlockSpec((1, q, D_SSD), bmap)],
        out_shape=[jax.ShapeDtypeStruct((bsz, t_all, D_SSD), F32)] * 2,
        scratch_shapes=[pltpu.VMEM((SSD_STATE, D_SSD), F32), pltpu.VMEM((SSD_STATE, D_SSD), F32)],
        compiler_params=pltpu.CompilerParams(dimension_semantics=("arbitrary", "arbitrary"),
                                             vmem_limit_bytes=VMEM_LIMIT),
        name="ssd",
    )(xbc, dt, xbc, dt, a_log_row)


def _block_transpose8(xs):
    lane = lax.broadcasted_iota(jnp.int32, (1, LANES), 1)
    xs = list(xs)
    for d in (4, 2, 1):
        keep = jnp.bitwise_and(lane, S5_CH * d) == 0
        nxt = list(xs)
        for a in range(8):
            if a & d:
                continue
            nxt[a] = jnp.where(keep, xs[a], pltpu.roll(xs[a + d], S5_CH * d, 1))
            nxt[a + d] = jnp.where(keep, pltpu.roll(xs[a], LANES - S5_CH * d, 1), xs[a + d])
        xs = nxt
    return xs


def _s5_kernel(u_ref, tt_ref, ws_ref, wo_ref, a_ref, d_ref, y_ref, s_ref, ub_ref,
               *, rows_blk, n_tiles, ctx_tiles):
    phase = pl.program_id(1)
    rb = pl.program_id(2)
    q = S5_Q
    gw = q * S5_CH
    pw = 2 * gw
    n_pairs = u_ref.shape[1] // (2 * S5_CH)
    r0 = pl.multiple_of(rb * rows_blk, 8)

    def token_rows(ref, j):
        return ref[pl.ds(j, rows_blk, stride=q), :]

    @pl.when(phase == 0)
    def _():
        uj = [token_rows(u_ref, j) for j in range(q)]
        ug = [_block_transpose8(uj[0:8]), _block_transpose8(uj[8:16])]
        for p in range(n_pairs):
            up = jnp.concatenate([ug[0][2 * p], ug[1][2 * p], ug[0][2 * p + 1], ug[1][2 * p + 1]],
                                 axis=1).astype(BF16)
            ub_ref[pl.ds(r0, rows_blk), p * pw:(p + 1) * pw] = up
            s_ref[pl.ds(r0, rows_blk), p * pw:(p + 1) * pw] = _dot(up, ws_ref[0, p])

    lo = lax.broadcasted_iota(jnp.int32, (8, LANES), 0) < 4
    hi = jnp.logical_not(lo)

    def roll4(v):
        return pltpu.roll(v, 4, 0)

    def tile_step(kt, cre, cim, ar, ai, col, first):
        r0 = pl.multiple_of(kt * 8, 8)
        sre = s_ref[pl.ds(r0, 8), col:col + LANES]
        sim = s_ref[pl.ds(r0, 8), col + LANES:col + 2 * LANES]
        h1re = roll4(ar * cre - ai * cim + sre)
        h1im = roll4(ar * cim + ai * cre + sim)
        s_ref[pl.ds(r0, 8), col:col + LANES] = jnp.where(first, cre, h1re)
        s_ref[pl.ds(r0, 8), col + LANES:col + 2 * LANES] = jnp.where(first, cim, h1im)
        t2re = ar * h1re - ai * h1im + sre
        t2im = ar * h1im + ai * h1re + sim
        second = jnp.logical_not(first)
        return jnp.where(second, t2re, roll4(t2re)), jnp.where(second, t2im, roll4(t2im))

    @pl.when(jnp.logical_and(phase == 1, rb == 0))
    def _():
        def arow(p, r):
            return jnp.broadcast_to(a_ref[0, p, r:r + 1, :], (8, LANES))

        coef = [[arow(p, r) for r in range(4)] for p in range(n_pairs)]

        def body(i, carry):
            kf = jnp.where(i < ctx_tiles, n_tiles - ctx_tiles + i, i - ctx_tiles)
            kb = n_tiles - 1 - i
            out = []
            for p in range(n_pairs):
                fre, fim, bre, bim = carry[4 * p:4 * p + 4]
                arf, aif, arb, aib = coef[p]
                fre, fim = tile_step(kf, fre, fim, arf, aif, p * pw, lo)
                bre, bim = tile_step(kb, bre, bim, arb, aib, p * pw + 2 * LANES, hi)
                out += [fre, fim, bre, bim]
            return tuple(out)

        zero = jnp.zeros((8, LANES), F32)
        lax.fori_loop(0, n_tiles, body, (zero,) * (4 * n_pairs))

    @pl.when(phase == 1)
    def _():
        yg = [[None] * 8, [None] * 8]
        for p in range(n_pairs):
            up = ub_ref[pl.ds(r0, rows_blk), p * pw:(p + 1) * pw]
            h = s_ref[pl.ds(r0, rows_blk), p * pw:(p + 1) * pw].astype(BF16)
            off = _dot(h, wo_ref[0, p])
            for gi in range(2):
                g = 2 * p + gi
                yp = _dot(up[:, gi * gw:(gi + 1) * gw], tt_ref[0, g]) + off[:, gi * gw:(gi + 1) * gw]
                yg[0][g] = yp[:, 0:LANES]
                yg[1][g] = yp[:, LANES:2 * LANES]
        for half in range(2):
            yi = _block_transpose8(yg[half])
            for il in range(8):
                i = 8 * half + il
                y_ref[pl.ds(i, rows_blk, stride=q), :] = yi[il] + token_rows(u_ref, i) * d_ref[0]


def _s5(u2d, tt, ws, wo, a16, dvec, layer, bsz, ctx_len):
    n_rows, width = u2d.shape
    rows = n_rows // S5_Q
    n_sg = width // LANES
    w0 = layer * n_sg
    n_blk = 4
    rows_blk = rows // n_blk
    assert rows_blk % 8 == 0
    n_tiles = rows // 8
    ctx_tiles = (ctx_len // S5_Q) * bsz // 8
    pairs_sg = S5_PAIRS // n_sg
    pw = 2 * S5_Q * S5_CH
    return pl.pallas_call(
        functools.partial(_s5_kernel, rows_blk=rows_blk, n_tiles=n_tiles, ctx_tiles=ctx_tiles),
        grid=(n_sg, 2, n_blk),
        in_specs=[
            pl.BlockSpec((rows_blk * S5_Q, LANES), lambda s, ph, rb: (rb, s)),
            pl.BlockSpec((1, 2 * pairs_sg, pw // 2, pw // 2), lambda s, ph, rb: (w0 + s, 0, 0, 0)),
            pl.BlockSpec((1, pairs_sg, pw, pw), lambda s, ph, rb: (w0 + s, 0, 0, 0)),
            pl.BlockSpec((1, pairs_sg, pw, pw), lambda s, ph, rb: (w0 + s, 0, 0, 0)),
            pl.BlockSpec((1, pairs_sg, 8, LANES), lambda s, ph, rb: (w0 + s, 0, 0, 0)),
            pl.BlockSpec((1, 1, LANES), lambda s, ph, rb: (w0 + s, 0, 0)),
        ],
        out_specs=pl.BlockSpec((rows_blk * S5_Q, LANES), lambda s, ph, rb: (ph * rb, s)),
        out_shape=jax.ShapeDtypeStruct(u2d.shape, F32),
        scratch_shapes=[pltpu.VMEM((rows, pairs_sg * pw), F32), pltpu.VMEM((rows, pairs_sg * pw), BF16)],
        compiler_params=pltpu.CompilerParams(dimension_semantics=("arbitrary", "arbitrary", "arbitrary"),
                                             vmem_limit_bytes=VMEM_LIMIT),
        name="s5",
    )(u2d, tt, ws, wo, a16, dvec)


def _cmul(ar, ai, br, bi):
    return ar * br - ai * bi, ar * bi + ai * br


def _cpow(br, bi, expo, nbits):
    rr = jnp.ones_like(br)
    ri = jnp.zeros_like(br)
    for bit in range(nbits):
        take = jnp.bitwise_and(lax.shift_right_logical(expo, bit), 1) == 1
        nr, ni = _cmul(rr, ri, br, bi)
        rr = jnp.where(take, nr, rr)
        ri = jnp.where(take, ni, ri)
        if bit + 1 < nbits:
            br, bi = _cmul(br, bi, br, bi)
    return rr, ri


def _s5_prep_kernel(rows_ref, lcol_ref, bt_ref, ct_ref, tt_ref, ws_ref, wo_ref, a_ref):
    q, ch, p = S5_Q, S5_CH, S5_STATE
    gw, pw = q * ch, 2 * q * ch
    lam_row = [(rows_ref[0, 2 * d:2 * d + 1, :], rows_ref[0, 2 * d + 1:2 * d + 2, :]) for d in range(2)]
    f_row = [(rows_ref[0, 4 + 2 * d:5 + 2 * d, :], rows_ref[0, 5 + 2 * d:6 + 2 * d, :]) for d in range(2)]
    bt_r, bt_i = bt_ref[0, 0], bt_ref[0, 1]
    ct_r, ct_i = ct_ref[0, 0], ct_ref[0, 1]

    ri = lax.broadcasted_iota(jnp.int32, (pw, 2 * p), 0)
    li = lax.broadcasted_iota(jnp.int32, (pw, 2 * p), 1)
    same = lax.shift_right_logical(ri, 8) == lax.shift_right_logical(li, 6)
    j = jnp.bitwise_and(lax.shift_right_logical(ri, 4), q - 1)
    b2_r = jnp.where(same, jnp.concatenate([bt_r] * (pw // ch), axis=0), 0.0)
    b2_i = jnp.where(same, jnp.concatenate([bt_i] * (pw // ch), axis=0), 0.0)
    blocks = []
    for d in range(2):
        expo = (q - 1 - j) if d == 0 else j
        pr, pi = _cpow(jnp.broadcast_to(lam_row[d][0], (pw, 2 * p)), jnp.broadcast_to(lam_row[d][1], (pw, 2 * p)),
                       expo, 4)
        gr, gi = _cmul(pr, pi, f_row[d][0], f_row[d][1])
        wr, wi = _cmul(b2_r, b2_i, gr, gi)
        blocks += [wr, wi]
    ws_ref[0] = jnp.concatenate(blocks, axis=1).astype(BF16)

    ri = lax.broadcasted_iota(jnp.int32, (2 * p, pw), 0)
    li = lax.broadcasted_iota(jnp.int32, (2 * p, pw), 1)
    row_g = lax.shift_right_logical(ri, 6)
    slot = lax.shift_right_logical(li, 4)
    same = row_g == lax.shift_right_logical(li, 8)
    i = jnp.bitwise_and(slot, q - 1)
    lam_col = [(jnp.concatenate([lcol_ref[0, 2 * d]] * (pw // LANES), axis=1),
                jnp.concatenate([lcol_ref[0, 2 * d + 1]] * (pw // LANES), axis=1)) for d in range(2)]
    blocks = []
    for d in range(2):
        expo = (i + 1) if d == 0 else (q - i)
        pr, pi = _cpow(lam_col[d][0], lam_col[d][1], expo, 5)
        cr, ci = _cmul(ct_r, ct_i, pr, pi)
        blocks += [jnp.where(same, cr, 0.0), jnp.where(same, -ci, 0.0)]
    wo_ref[0] = jnp.concatenate(blocks, axis=0).astype(BF16)

    fr, fi = _cpow(lam_col[0][0], lam_col[0][1], jnp.maximum(slot - (q - 1), 0), 4)
    br, bi = _cpow(lam_col[1][0], lam_col[1][1], jnp.maximum(q - 1 - slot, 0), 4)
    mf = _cmul(ct_r, ct_i, fr, fi)
    mb = _cmul(ct_r, ct_i, br, bi)
    use_f = jnp.logical_and(slot >= q - 1, slot <= 2 * q - 2)
    use_b = slot <= q - 1
    bb = [_cmul(bt_r, bt_i, f_row[d][0], f_row[d][1]) for d in range(2)]
    lhs = jnp.concatenate([bb[0][0], bb[0][1], bb[1][0], bb[1][1]], axis=1)
    for g in range(2):
        mine = row_g == g
        rhs = jnp.concatenate([jnp.where(jnp.logical_and(mine, use_f), mf[0], 0.0),
                               jnp.where(jnp.logical_and(mine, use_f), -mf[1], 0.0),
                               jnp.where(jnp.logical_and(mine, use_b), mb[0], 0.0),
                               jnp.where(jnp.logical_and(mine, use_b), -mb[1], 0.0)], axis=0)
        z = jnp.dot(lhs, rhs, preferred_element_type=F32, precision=lax.Precision.HIGHEST)
        for jb in range(q):
            shift = (pw - ch * (q - 1 - jb)) % pw
            win = z if shift == 0 else pltpu.roll(z, shift, 1)
            tt_ref[0, g, jb * ch:(jb + 1) * ch, :] = win[:, 0:gw].astype(BF16)

    out = []
    for d in range(2):
        ar, ai = lam_row[d]
        for _ in range(4):
            ar, ai = _cmul(ar, ai, ar, ai)
        out += [ar, ai]
    a_ref[0] = jnp.concatenate(out + [jnp.zeros((4, 2 * p), F32)], axis=0)


def _s5_weights(a_re, a_im, log_dt, b_re, b_im, c_re, c_im):
    depth = a_re.shape[0]
    q, ch, p, npair = S5_Q, S5_CH, S5_STATE, S5_PAIRS
    lr, li = a_re.astype(F32), a_im.astype(F32)
    step = jnp.exp(log_dt.astype(F32))[..., None]
    mag = jnp.exp(lr * step)
    lbr, lbi = mag * jnp.cos(li * step), mag * jnp.sin(li * step)
    den = lr * lr + li * li
    fr, fi = ((lbr - 1.0) * lr + lbi * li) / den, (lbi * lr - (lbr - 1.0) * li) / den

    def pair_rows(v):
        return v.reshape(depth, 2, npair, 2 * p).transpose(0, 2, 1, 3).reshape(depth * npair, 2, 2 * p)

    vals = [pair_rows(v) for v in (lbr, lbi, fr, fi)]
    rows = jnp.stack([vals[0][:, 0], vals[1][:, 0], vals[0][:, 1], vals[1][:, 1],
                      vals[2][:, 0], vals[3][:, 0], vals[2][:, 1], vals[3][:, 1]], axis=1)
    lcol = jnp.broadcast_to(rows[:, 0:4, :, None], (depth * npair, 4, 2 * p, LANES))

    def b_t(b):
        return b.astype(F32).reshape(depth * npair, 2, p, ch).transpose(0, 3, 1, 2).reshape(depth * npair, ch, 2 * p)

    def c_t(c):
        ct = c.astype(F32).reshape(depth * npair, 2, ch, p).transpose(0, 1, 3, 2).reshape(depth * npair, 2 * p, ch)
        return jnp.tile(ct, (1, 1, 2 * q))

    bt = jnp.stack([b_t(b_re), b_t(b_im)], axis=1)
    ct = jnp.stack([c_t(c_re), c_t(c_im)], axis=1)
    n = depth * npair
    gw, pw = q * ch, 2 * q * ch
    return pl.pallas_call(
        _s5_prep_kernel,
        grid=(n,),
        in_specs=[
            pl.BlockSpec((1, 8, 2 * p), lambda i: (i, 0, 0)),
            pl.BlockSpec((1, 4, 2 * p, LANES), lambda i: (i, 0, 0, 0)),
            pl.BlockSpec((1, 2, ch, 2 * p), lambda i: (i, 0, 0, 0)),
            pl.BlockSpec((1, 2, 2 * p, pw), lambda i: (i, 0, 0, 0)),
        ],
        out_specs=[
            pl.BlockSpec((1, 2, gw, gw), lambda i: (i, 0, 0, 0)),
            pl.BlockSpec((1, pw, pw), lambda i: (i, 0, 0)),
            pl.BlockSpec((1, pw, pw), lambda i: (i, 0, 0)),
            pl.BlockSpec((1, 8, 2 * p), lambda i: (i, 0, 0)),
        ],
        out_shape=[
            jax.ShapeDtypeStruct((n, 2, gw, gw), BF16),
            jax.ShapeDtypeStruct((n, pw, pw), BF16),
            jax.ShapeDtypeStruct((n, pw, pw), BF16),
            jax.ShapeDtypeStruct((n, 8, 2 * p), F32),
        ],
        compiler_params=pltpu.CompilerParams(vmem_limit_bytes=VMEM_LIMIT),
        name="s5_prep",
    )(rows, lcol, bt, ct)


def _tail_kernel(*refs, period, aliased):
    (x_ref, yf_ref, yb_ref, xs_ref, z_ref, y5_ref, mod_ref, dv_ref, ng_ref, gw_ref, gb_ref,
     wo_ref, gpm_ref, gpf_ref, wu_ref, cw_ref, cb_ref, wd_ref, gqf_ref) = refs[:19]
    o_ref, act_ref = refs[-2:]
    assert len(refs) == 21 + int(aliased)
    tm = x_ref.shape[1]
    d = D_MODEL
    x_tile = x_ref[0]
    mod = mod_ref[0]
    gate1, shift2, scale2, gate2 = (mod[:, 2 * d:3 * d], mod[:, 3 * d:4 * d],
                                    mod[:, 4 * d:5 * d], mod[:, 5 * d:6 * d])

    y_ssd = (yf_ref[0] + yb_ref[0] + xs_ref[0] * dv_ref[...]) * _silu(z_ref[0])
    y_ssd = _rms(y_ssd, ng_ref[...])
    g5 = _gelu_tanh(y5_ref[...].reshape(tm, D_S5))
    y_s5 = g5 * _sigmoid(_dot(g5.astype(BF16), gw_ref[...]) + gb_ref[...])
    mix = _dot(y_ssd.astype(BF16), wo_ref[0:D_SSD, :]) + _dot(y_s5.astype(BF16), wo_ref[D_SSD:, :])
    x1 = x_tile + gate1 * _rms(mix, gpm_ref[...])

    h2 = (_rms(x1, gpf_ref[...]) * (1.0 + scale2) + shift2).astype(BF16)
    run = period + 1
    r8 = lax.broadcasted_iota(jnp.int32, (8, FFN_COLS), 0)

    def zero_edge(a, row):
        pieces = []
        for s in range(0, tm, run):
            if row == 0:
                pieces += [jnp.where(r8 == 0, 0.0, a[s:s + 8]), a[s + 8:s + run]]
            else:
                pieces += [a[s:s + run - 8], jnp.where(r8 == 7, 0.0, a[s + run - 8:s + run])]
        return jnp.concatenate(pieces, axis=0)

    def conv(v, col):
        left = zero_edge(pltpu.roll(v, 1, 0), 0)
        right = zero_edge(pltpu.roll(v, tm - 1, 0), 7)
        w = cw_ref[:, col:col + FFN_COLS]
        return left * w[0:1, :] + v * w[1:2, :] + right * w[2:3, :] + cb_ref[:, col:col + FFN_COLS]

    for c in range(D_FF // FFN_COLS):
        c0 = c * FFN_COLS
        gate = conv(_dot(h2, wu_ref[:, c0:c0 + FFN_COLS]), c0)
        val = conv(_dot(h2, wu_ref[:, D_FF + c0:D_FF + c0 + FFN_COLS]), D_FF + c0)
        act_ref[:, c0:c0 + FFN_COLS] = (_silu(gate) * val).astype(BF16)
    o_ref[0] = x1 + gate2 * _rms(_dot(act_ref[...], wd_ref[...]), gqf_ref[...])


def _tail(xa, ca, ctx_blk, yf, yb, xbc, z, y5, mod, layer, p, seq, ctx_len, with_context):
    bsz = xa.shape[0]
    t_out = seq + (ctx_len if with_context else 0)
    weights = [
        _full((1, D_SSD)), _full((1, D_SSD)),
        _resident((D_S5, D_S5)), _full((1, D_S5)),
        _resident((D_MODEL, D_MODEL)),
        _full((1, D_MODEL)), _full((1, D_MODEL)),
        _resident((D_MODEL, 2 * D_FF)),
        _full((3, 2 * D_FF)), _full((1, 2 * D_FF)),
        _resident((D_FF, D_MODEL)),
        _full((1, D_MODEL)),
    ]
    w_args = (p['ssd_d'], p['ssd_norm_g'], p['glu_w'], p['glu_b'], p['w_out'], p['g_post_mix'],
              p['g_pre_ffn'], p['w_up'], p['ffn_conv_w'], p['ffn_conv_b'], p['w_down'], p['g_post_ffn'])
    out_shape = jax.ShapeDtypeStruct((bsz, t_out, D_MODEL), F32)
    params = pltpu.CompilerParams(vmem_limit_bytes=VMEM_LIMIT)

    tm = LATENT_TILE
    tok = lambda w: pl.BlockSpec((1, tm, w), lambda b, t: (b, t, 0))
    out = pl.pallas_call(
        functools.partial(_tail_kernel, period=GRID_W - 1, aliased=False),
        grid=(bsz, seq // tm),
        in_specs=[
            tok(D_MODEL), tok(D_SSD), tok(D_SSD), tok(D_SSD), tok(D_SSD),
            pl.BlockSpec((tm // S5_Q, S5_Q, D_S5), lambda b, t: (t, b, 0)),
            pl.BlockSpec((1, 1, 6 * D_MODEL), lambda b, t: (layer * MOD_ROWS + b, 0, 0)),
        ] + weights,
        out_specs=tok(D_MODEL),
        out_shape=out_shape,
        scratch_shapes=[pltpu.VMEM((tm, D_FF), BF16)],
        compiler_params=params,
        name="tail",
    )(xa, yf, yb, xbc, z, y5, mod, *w_args)
    if not with_context:
        return out

    tc = ctx_len
    c0 = seq // tc
    ctok = lambda w: pl.BlockSpec((1, tc, w), lambda b, t: (b, c0, 0))
    return pl.pallas_call(
        functools.partial(_tail_kernel, period=tc - 1, aliased=True),
        grid=(bsz, 1),
        in_specs=[
            pl.BlockSpec((1, tc, D_MODEL), lambda b, t: (b, ctx_blk, 0)),
            ctok(D_SSD), ctok(D_SSD), ctok(D_SSD), ctok(D_SSD),
            pl.BlockSpec((tc // S5_Q, S5_Q, D_S5), lambda b, t: (c0, b, 0)),
            pl.BlockSpec((1, 1, 6 * D_MODEL), lambda b, t: (layer * MOD_ROWS + bsz, 0, 0)),
        ] + weights + [pl.BlockSpec(memory_space=pl.ANY)],
        out_specs=ctok(D_MODEL),
        out_shape=out_shape,
        scratch_shapes=[pltpu.VMEM((tc, D_FF), BF16)],
        input_output_aliases={19: 0},
        compiler_params=params,
        name="tail_ctx",
    )(ca, yf, yb, xbc, z, y5, mod, *w_args, out)


def _row(v, width=None):
    v = v.astype(F32).reshape(1, -1)
    if width is not None and v.shape[1] < width:
        v = jnp.pad(v, ((0, 0), (0, width - v.shape[1])))
    return v


def kernel(x, c, ctx, c_ctx, w_ada, b_ada, g_pre_mix, g_post_mix, g_pre_ffn, g_post_ffn, w_in, ssd_conv_w,
           ssd_conv_b, ssd_dt_bias, ssd_a_log, ssd_d, ssd_norm_g, s5_a_re, s5_a_im, s5_log_dt, s5_b_re,
           s5_b_im, s5_c_re, s5_c_im, s5_d, s5_glu_w, s5_glu_b, w_out, ffn_w_up, ffn_conv_w, ffn_conv_b,
           ffn_w_down):
    bsz, seq, _ = x.shape
    ctx_len = ctx.shape[1]
    depth = w_in.shape[0]
    assert seq % LATENT_TILE == 0 and seq % ctx_len == 0 and ctx_len % SSD_CHUNK == 0
    assert bsz + 1 <= MOD_ROWS and bsz * 2 == 8

    cvec = jnp.concatenate([c, c_ctx[None, :], jnp.zeros((MOD_ROWS - bsz - 1, D_MODEL), F32)], axis=0)
    mod = _ada_table(cvec, w_ada, b_ada).reshape(depth * MOD_ROWS, 1, 6 * D_MODEL)
    xa, ca, ctx_blk = x, ctx, 0

    n_sg = D_S5 // LANES
    psg = S5_PAIRS // n_sg
    gw = S5_Q * S5_CH
    tt, ws, wo, a16 = _s5_weights(s5_a_re, s5_a_im, s5_log_dt, s5_b_re, s5_b_im, s5_c_re, s5_c_im)
    tt = tt.reshape(depth * n_sg, 2 * psg, gw, gw)
    ws = ws.reshape(depth * n_sg, psg, 2 * gw, 2 * gw)
    wo = wo.reshape(depth * n_sg, psg, 2 * gw, 2 * gw)
    a16 = a16.reshape(depth * n_sg, psg, 8, LANES)
    dvec = s5_d.astype(F32).reshape(depth * n_sg, 1, LANES)

    z0, z1, z2 = D_SSD, D_SSD + XBC_DIM, D_SSD + XBC_DIM + 2 * SSD_HEADS
    for l in range(depth):
        last = l == depth - 1
        wl = w_in[l]
        w_in_l = jnp.concatenate([wl[:, :z0], wl[:, z0:z1], wl[:, z2:], wl[:, z1:z2],
                                  jnp.zeros((D_MODEL, LANES - 2 * SSD_HEADS), F32)], axis=1).astype(BF16)
        z, xbc, dt, u = _inproj(xa, ca, ctx_blk, mod, l, _row(g_pre_mix[l]), w_in_l, ssd_conv_w[l].astype(F32),
                                _row(ssd_conv_b[l]), _row(ssd_dt_bias[l], LANES), seq, ctx_len)
        yf, yb = _ssd(xbc, dt, _row(ssd_a_log[l], LANES), ctx_len)
        y5 = _s5(u.reshape(-1, D_S5), tt, ws, wo, a16, dvec, l, bsz, ctx_len).reshape(u.shape)
        p = {
            'ssd_d': _row(jnp.repeat(ssd_d[l], SSD_HEAD_DIM)), 'ssd_norm_g': _row(ssd_norm_g[l]),
            'glu_w': s5_glu_w[l].astype(BF16), 'glu_b': _row(s5_glu_b[l]), 'w_out': w_out[l].astype(BF16),
            'g_post_mix': _row(g_post_mix[l]), 'g_pre_ffn': _row(g_pre_ffn[l]),
            'w_up': ffn_w_up[l].astype(BF16), 'ffn_conv_w': ffn_conv_w[l].astype(F32),
            'ffn_conv_b': _row(ffn_conv_b[l]), 'w_down': ffn_w_down[l].astype(BF16),
            'g_post_ffn': _row(g_post_ffn[l]),
        }
        xa = _tail(xa, ca, ctx_blk, yf, yb, xbc, z, y5, mod, l, p, seq, ctx_len, with_context=not last)
        ca, ctx_blk = xa, seq // ctx_len
    return xa
```

```python
import functools

import jax
import jax.numpy as jnp
from jax import lax
from jax.experimental import pallas as pl
from jax.experimental.pallas import tpu as pltpu

F32 = jnp.float32
BF16 = jnp.bfloat16

D_MODEL = 1024
D_SSD = 512
SSD_HEADS = 8
SSD_HEAD_DIM = 64
SSD_GROUPS = 2
SSD_STATE = 128
SSD_CHUNK = 128
SSD_BLOCK = 256
XBC_DIM = D_SSD + 2 * SSD_GROUPS * SSD_STATE
D_S5 = 512
S5_CH = 16
S5_GROUPS = 32
S5_STATE = 64
S5_Q = 16
S5_PAIRS = S5_GROUPS // 2
D_FF = 2816
FFN_COLS = 256
GRID_W = 64
NORM_EPS = 1e-6
LOG2_E = 1.4426950408889634
LANES = 128
LATENT_TILE = 512
W_IN_COLS = D_SSD + XBC_DIM + D_S5 + LANES
MOD_ROWS = 8
VMEM_LIMIT = 56 * 1024 * 1024


def _silu(x):
    return x * (1.0 / (1.0 + jnp.exp(-x)))


def _sigmoid(x):
    return 1.0 / (1.0 + jnp.exp(-x))


def _gelu_tanh(x):
    c = 0.7978845608028654
    return x * (0.5 * (1.0 + jnp.tanh(c * (x + 0.044715 * (x * x * x)))))


def _softplus(x):
    return jnp.maximum(x, 0.0) + jnp.log(1.0 + jnp.exp(-jnp.abs(x)))


def _rms(x, g):
    ms = jnp.mean(x * x, axis=-1, keepdims=True)
    return x * lax.rsqrt(ms + NORM_EPS) * g


def _dot(a, b):
    return jnp.dot(a, b, preferred_element_type=F32)


def _full(shape):
    n = len(shape)
    return pl.BlockSpec(shape, lambda *_: (0,) * n)


def _resident(shape, layer):
    n = len(shape)
    return pl.BlockSpec((None,) + tuple(shape), lambda *_: (layer,) + (0,) * n, pipeline_mode=pl.Buffered(1))


def _ada_kernel(c_ref, w_ref, b_ref, o_ref):
    c = _silu(c_ref[...]).astype(BF16)
    o_ref[0] = _dot(c, w_ref[0].astype(BF16)) + b_ref[0]


def _ada_table(cvec, w_ada, b_ada):
    depth = w_ada.shape[0]
    n_col = 6 * D_MODEL // D_MODEL
    return pl.pallas_call(
        _ada_kernel,
        grid=(depth, n_col),
        in_specs=[
            _full((MOD_ROWS, D_MODEL)),
            pl.BlockSpec((1, D_MODEL, D_MODEL), lambda l, n: (l, 0, n)),
            pl.BlockSpec((1, 1, D_MODEL), lambda l, n: (l, 0, n)),
        ],
        out_specs=pl.BlockSpec((1, MOD_ROWS, D_MODEL), lambda l, n: (l, 0, n)),
        out_shape=jax.ShapeDtypeStruct((depth, MOD_ROWS, 6 * D_MODEL), F32),
        name="ada_table",
    )(cvec, w_ada, b_ada.reshape(depth, 1, 6 * D_MODEL))


def _inproj_kernel(*refs, halo):
    if halo:
        (x_ref, xp_ref, xn_ref, mod_ref, g_ref, w_ref, cw_ref, cb_ref, dtb_ref,
         z_ref, xbc_ref, dt_ref, u_ref, h_ref) = refs
    else:
        (x_ref, mod_ref, g_ref, w_ref, cw_ref, cb_ref, dtb_ref, _, _, _, _,
         z_ref, xbc_ref, dt_ref, u_ref, h_ref) = refs
    t = pl.program_id(1)
    nt = pl.num_programs(1)
    tm = x_ref.shape[1]
    x_tile = x_ref[0]
    g = g_ref[...]
    mod = mod_ref[0]
    shift = mod[:, 0:D_MODEL]
    scale = mod[:, D_MODEL:2 * D_MODEL]

    def prenorm(xv):
        return (_rms(xv, g) * (1.0 + scale) + shift).astype(BF16)

    h_ref[...] = prenorm(x_tile)
    c_xbc, c_u, c_dt = D_SSD, D_SSD + XBC_DIM, D_SSD + XBC_DIM + D_S5
    w_xbc = w_ref[:, c_xbc:c_u]

    if halo:
        edge_prev = _dot(prenorm(xp_ref[0]), w_xbc)[7:8, :] * (t != 0).astype(F32)
        edge_next = _dot(prenorm(xn_ref[0]), w_xbc)[0:1, :] * (t != nt - 1).astype(F32)
        edge_prev = jnp.broadcast_to(edge_prev, (8, XBC_DIM))
        edge_next = jnp.broadcast_to(edge_next, (8, XBC_DIM))
    else:
        edge_prev = edge_next = jnp.zeros((8, XBC_DIM), F32)
    xbc = _dot(h_ref[...], w_xbc)
    z_ref[0] = _dot(h_ref[...], w_ref[:, 0:c_xbc])
    u_ref[...] = _dot(h_ref[...], w_ref[:, c_u:c_dt]).reshape(tm // S5_Q, S5_Q, D_S5)
    dt_ref[0] = _softplus(_dot(h_ref[...], w_ref[:, c_dt:]) + dtb_ref[...])
    r8 = lax.broadcasted_iota(jnp.int32, (8, XBC_DIM), 0)
    left = pltpu.roll(xbc, 1, 0)
    left = jnp.concatenate([jnp.where(r8 == 0, edge_prev, left[0:8]), left[8:]], axis=0)
    right = pltpu.roll(xbc, tm - 1, 0)
    right = jnp.concatenate([right[:tm - 8], jnp.where(r8 == 7, edge_next, right[tm - 8:])], axis=0)
    conv = left * cw_ref[0:1, :] + xbc * cw_ref[1:2, :] + right * cw_ref[2:3, :] + cb_ref[...]
    xbc_ref[0] = _silu(conv)


def _inproj(xa, ca, ctx_blk, mod, layer, g_pre, w_in, conv_w, conv_b, dt_bias, seq, ctx_len):
    bsz = xa.shape[0]
    t_all = seq + ctx_len
    tm = LATENT_TILE
    rows8 = tm // 8
    last8 = seq // 8 - 1
    weights = [_full((1, D_MODEL)), _resident((D_MODEL, W_IN_COLS), layer), _full((3, XBC_DIM)),
               _full((1, XBC_DIM)), _full((1, LANES))]
    w_args = (g_pre, w_in, conv_w, conv_b, dt_bias)
    out_shape = [
        jax.ShapeDtypeStruct((bsz, t_all, D_SSD), F32),
        jax.ShapeDtypeStruct((bsz, t_all, XBC_DIM), F32),
        jax.ShapeDtypeStruct((bsz, t_all, LANES), F32),
        jax.ShapeDtypeStruct((t_all // S5_Q, bsz * S5_Q, D_S5), F32),
    ]
    params = pltpu.CompilerParams(vmem_limit_bytes=VMEM_LIMIT)

    tok = lambda w: pl.BlockSpec((1, tm, w), lambda b, t: (b, t, 0))
    outs = pl.pallas_call(
        functools.partial(_inproj_kernel, halo=True),
        grid=(bsz, seq // tm),
        in_specs=[
            tok(D_MODEL),
            pl.BlockSpec((1, 8, D_MODEL), lambda b, t: (b, jnp.maximum(t * rows8 - 1, 0), 0)),
            pl.BlockSpec((1, 8, D_MODEL), lambda b, t: (b, jnp.minimum((t + 1) * rows8, last8), 0)),
            pl.BlockSpec((1, 1, 6 * D_MODEL), lambda b, t: (layer * MOD_ROWS + b, 0, 0)),
        ] + weights,
        out_specs=[tok(D_SSD), tok(XBC_DIM), tok(LANES),
                   pl.BlockSpec((tm // S5_Q, S5_Q, D_S5), lambda b, t: (t, b, 0))],
        out_shape=out_shape,
        scratch_shapes=[pltpu.VMEM((tm, D_MODEL), BF16)],
        compiler_params=params,
        name="inproj",
    )(xa, xa, xa, mod, *w_args)

    tc = ctx_len
    c0 = seq // tc
    ctok = lambda w: pl.BlockSpec((1, tc, w), lambda b, t: (b, c0, 0))
    anyspec = pl.BlockSpec(memory_space=pl.ANY)
    return pl.pallas_call(
        functools.partial(_inproj_kernel, halo=False),
        grid=(bsz, 1),
        in_specs=[
            pl.BlockSpec((1, tc, D_MODEL), lambda b, t: (b, ctx_blk, 0)),
            pl.BlockSpec((1, 1, 6 * D_MODEL), lambda b, t: (layer * MOD_ROWS + bsz, 0, 0)),
        ] + weights + [anyspec] * 4,
        out_specs=[ctok(D_SSD), ctok(XBC_DIM), ctok(LANES),
                   pl.BlockSpec((tc // S5_Q, S5_Q, D_S5), lambda b, t: (c0, b, 0))],
        out_shape=out_shape,
        scratch_shapes=[pltpu.VMEM((tc, D_MODEL), BF16)],
        input_output_aliases={7: 0, 8: 1, 9: 2, 10: 3},
        compiler_params=params,
        name="inproj_ctx",
    )(ca, mod, *w_args, *outs)


def _split3(x):
    hi = x.astype(BF16)
    r1 = x - hi.astype(F32)
    mid = r1.astype(BF16)
    lo = (r1 - mid.astype(F32)).astype(BF16)
    return hi, mid, lo


def _ssd_prologue(d, x_ref, dt_ref, a_row, r0):
    q = SSD_CHUNK
    fwd = d == 0
    ri = lax.broadcasted_iota(jnp.int32, (q, q), 0)
    ci = lax.broadcasted_iota(jnp.int32, (q, q), 1)
    mask = (ri >= ci) if fwd else (ri <= ci)
    end = q - 1 if fwd else 0
    xbc = x_ref[0, r0:r0 + q, :]
    dt = dt_ref[0, r0:r0 + q, :]
    tri = mask.astype(BF16)
    hi, mid, lo = _split3(dt * a_row)
    cum = (_dot(tri, hi) + _dot(tri, mid) + _dot(tri, lo)) * LOG2_E
    bm, cm, cb = [], [], []
    for g in range(SSD_GROUPS):
        bm.append(xbc[:, D_SSD + g * SSD_STATE:D_SSD + (g + 1) * SSD_STATE])
        cm.append(xbc[:, D_SSD + (SSD_GROUPS + g) * SSD_STATE:D_SSD + (SSD_GROUPS + g + 1) * SSD_STATE])
        cb.append(lax.dot_general(cm[g].astype(BF16), bm[g].astype(BF16), (((1,), (1,)), ((), ())),
                                  preferred_element_type=F32))
    return dict(d=d, r0=r0, mask=mask, end=end, cum=cum, cum_t=cum.T, dt_t=dt.T,
                etot=jnp.exp2(cum[end:end + 1, :]), xs_b=xbc[:, 0:D_SSD].astype(BF16),
                cm=cm, cb=cb, bm_t=[b.T for b in bm])


def _ssd_pair(c, pair, st_pair, y_ref):
    q = SSD_CHUNK
    d, r0, end, g = c['d'], c['r0'], c['end'], pair // 2
    lo64 = lax.broadcasted_iota(jnp.int32, (1, LANES), 1) < SSD_HEAD_DIM
    sl = slice(pair * LANES, (pair + 1) * LANES)
    xs_pair = c['xs_b'][:, sl]
    st_pair_b = st_pair.astype(BF16)
    acc = None
    new = None
    for hh in range(2):
        lane = SSD_HEADS * d + 2 * pair + hh
        crow = c['cum_t'][lane:lane + 1, :]
        drow = c['dt_t'][lane:lane + 1, :]
        cl = jnp.broadcast_to(c['cum'][:, lane:lane + 1], (q, q))
        decay = jnp.exp2(jnp.where(c['mask'], cl - crow, -jnp.inf))
        m = c['cb'][g] * decay * drow
        cme = c['cm'][g] * jnp.exp2(cl)
        lhs = jnp.concatenate([m, cme], axis=1).astype(BF16)
        sel = lo64 if hh == 0 else jnp.logical_not(lo64)
        xs_h = jnp.where(sel, xs_pair, jnp.zeros_like(xs_pair))
        rhs = jnp.concatenate([xs_h, jnp.where(sel, st_pair_b, jnp.zeros_like(st_pair_b))], axis=0)
        part = _dot(lhs, rhs)
        acc = part if acc is None else acc + part
        wrow = drow * jnp.exp2(crow[:, end:end + 1] - crow)
        part = _dot((c['bm_t'][g] * wrow).astype(BF16), xs_h)
        new = part if new is None else new + part
    y_ref[0, r0:r0 + q, sl] = acc
    l0 = SSD_HEADS * d + 2 * pair
    etot = c['etot']
    dpair = jnp.where(lo64, jnp.broadcast_to(etot[:, l0:l0 + 1], (1, LANES)),
                      jnp.broadcast_to(etot[:, l0 + 1:l0 + 2], (1, LANES)))
    return st_pair * dpair + new


def _ssd_kernel(xf_ref, dtf_ref, xb_ref, dtb_ref, alog_ref, yf_ref, yb_ref, stf_ref, stb_ref):
    @pl.when(pl.program_id(1) == 0)
    def _():
        stf_ref[...] = jnp.zeros_like(stf_ref)
        stb_ref[...] = jnp.zeros_like(stb_ref)

    a_row = -jnp.exp(alog_ref[...])
    n_sub = xf_ref.shape[1] // SSD_CHUNK
    n_pair = SSD_HEADS // 2
    pf = [_ssd_prologue(0, xf_ref, dtf_ref, a_row, s * SSD_CHUNK) for s in range(n_sub)]
    pb = [_ssd_prologue(1, xb_ref, dtb_ref, a_row, (n_sub - 1 - s) * SSD_CHUNK) for s in range(n_sub)]
    sf = [stf_ref[:, p * LANES:(p + 1) * LANES] for p in range(n_pair)]
    sb = [stb_ref[:, p * LANES:(p + 1) * LANES] for p in range(n_pair)]
    for s in range(n_sub):
        for p in range(n_pair):
            sf[p] = _ssd_pair(pf[s], p, sf[p], yf_ref)
            sb[p] = _ssd_pair(pb[s], p, sb[p], yb_ref)
    for p in range(n_pair):
        stf_ref[:, p * LANES:(p + 1) * LANES] = sf[p]
        stb_ref[:, p * LANES:(p + 1) * LANES] = sb[p]


def _ssd(xbc, dt, a_log_row, ctx_len):
    bsz, t_all, _ = xbc.shape
    q = SSD_BLOCK
    assert ctx_len % q == 0 and t_all % q == 0
    nch = t_all // q
    ncc = ctx_len // q
    nlc = nch - ncc

    def fmap(b, i):
        return (b, jnp.where(i < ncc, nlc + i, i - ncc), 0)

    def bmap(b, i):
        return (b, nch - 1 - i, 0)

    return pl.pallas_call(
        _ssd_kernel,
        grid=(bsz, nch),
        in_specs=[
            pl.BlockSpec((1, q, XBC_DIM), fmap),
            pl.BlockSpec((1, q, LANES), fmap),
            pl.BlockSpec((1, q, XBC_DIM), bmap),
            pl.BlockSpec((1, q, LANES), bmap),
            _full((1, LANES)),
        ],
        out_specs=[pl.BlockSpec((1, q, D_SSD), fmap), pl.BlockSpec((1, q, D_SSD), bmap)],
        out_shape=[jax.ShapeDtypeStruct((bsz, t_all, D_SSD), F32)] * 2,
        scratch_shapes=[pltpu.VMEM((SSD_STATE, D_SSD), F32), pltpu.VMEM((SSD_STATE, D_SSD), F32)],
        compiler_params=pltpu.CompilerParams(dimension_semantics=("arbitrary", "arbitrary"),
                                             vmem_limit_bytes=VMEM_LIMIT),
        name="ssd",
    )(xbc, dt, xbc, dt, a_log_row)


def _block_transpose8(xs):
    lane = lax.broadcasted_iota(jnp.int32, (1, LANES), 1)
    xs = list(xs)
    for d in (4, 2, 1):
        keep = jnp.bitwise_and(lane, S5_CH * d) == 0
        nxt = list(xs)
        for a in range(8):
            if a & d:
                continue
            nxt[a] = jnp.where(keep, xs[a], pltpu.roll(xs[a + d], S5_CH * d, 1))
            nxt[a + d] = jnp.where(keep, pltpu.roll(xs[a], LANES - S5_CH * d, 1), xs[a + d])
        xs = nxt
    return xs


def _s5_kernel(u_ref, tt_ref, ws_ref, wo_ref, a_ref, d_ref, y_ref, s_ref, ub_ref,
               *, rows_blk, n_tiles, ctx_tiles):
    phase = pl.program_id(1)
    rb = pl.program_id(2)
    q = S5_Q
    gw = q * S5_CH
    pw = 2 * gw
    n_pairs = u_ref.shape[1] // (2 * S5_CH)
    r0 = pl.multiple_of(rb * rows_blk, 8)

    def token_rows(ref, j):
        return ref[pl.ds(j, rows_blk, stride=q), :]

    @pl.when(phase == 0)
    def _():
        uj = [token_rows(u_ref, j) for j in range(q)]
        ug = [_block_transpose8(uj[0:8]), _block_transpose8(uj[8:16])]
        for p in range(n_pairs):
            up = jnp.concatenate([ug[0][2 * p], ug[1][2 * p], ug[0][2 * p + 1], ug[1][2 * p + 1]],
                                 axis=1).astype(BF16)
            ub_ref[pl.ds(r0, rows_blk), p * pw:(p + 1) * pw] = up
            s_ref[pl.ds(r0, rows_blk), p * pw:(p + 1) * pw] = _dot(up, ws_ref[0, p])

    lo = lax.broadcasted_iota(jnp.int32, (8, LANES), 0) < 4
    hi = jnp.logical_not(lo)

    def roll4(v):
        return pltpu.roll(v, 4, 0)

    def tile_step(kt, cre, cim, a, a2, col, first):
        r0 = pl.multiple_of(kt * 8, 8)
        (ar, ai), (a2r, a2i) = a, a2
        sre = s_ref[pl.ds(r0, 8), col:col + LANES]
        sim = s_ref[pl.ds(r0, 8), col + LANES:col + 2 * LANES]
        swr, swi = roll4(sre), roll4(sim)
        gre = ar * swr - ai * swi + sre
        gim = ar * swi + ai * swr + sim
        s_ref[pl.ds(r0, 8), col:col + LANES] = jnp.where(first, roll4(cre), ar * cre - ai * cim + swr)
        s_ref[pl.ds(r0, 8), col + LANES:col + 2 * LANES] = jnp.where(first, roll4(cim), ar * cim + ai * cre + swi)
        return a2r * cre - a2i * cim + gre, a2r * cim + a2i * cre + gim

    @pl.when(jnp.logical_and(phase == 1, rb == 0))
    def _():
        def arow(p, r):
            return jnp.broadcast_to(a_ref[0, p, r:r + 1, :], (8, LANES))

        coef = []
        for p in range(n_pairs):
            af, ab = (arow(p, 0), arow(p, 1)), (arow(p, 2), arow(p, 3))
            coef.append((af, _cmul(*af, *af), ab, _cmul(*ab, *ab)))

        def body(i, carry):
            kf = jnp.where(i < ctx_tiles, n_tiles - ctx_tiles + i, i - ctx_tiles)
            kb = n_tiles - 1 - i
            out = []
            for p in range(n_pairs):
                fre, fim, bre, bim = carry[4 * p:4 * p + 4]
                af, af2, ab, ab2 = coef[p]
                fre, fim = tile_step(kf, fre, fim, af, af2, p * pw, lo)
                bre, bim = tile_step(kb, bre, bim, ab, ab2, p * pw + 2 * LANES, hi)
                out += [fre, fim, bre, bim]
            return tuple(out)

        zero = jnp.zeros((8, LANES), F32)
        lax.fori_loop(0, n_tiles, body, (zero,) * (4 * n_pairs))

    @pl.when(phase == 1)
    def _():
        yg = [[None] * 8, [None] * 8]
        for p in range(n_pairs):
            up = ub_ref[pl.ds(r0, rows_blk), p * pw:(p + 1) * pw]
            h = s_ref[pl.ds(r0, rows_blk), p * pw:(p + 1) * pw].astype(BF16)
            off = _dot(h, wo_ref[0, p])
            for gi in range(2):
                g = 2 * p + gi
                yp = _dot(up[:, gi * gw:(gi + 1) * gw], tt_ref[0, g]) + off[:, gi * gw:(gi + 1) * gw]
                yg[0][g] = yp[:, 0:LANES]
                yg[1][g] = yp[:, LANES:2 * LANES]
        for half in range(2):
            yi = _block_transpose8(yg[half])
            for il in range(8):
                i = 8 * half + il
                y_ref[pl.ds(i, rows_blk, stride=q), :] = yi[il] + token_rows(u_ref, i) * d_ref[0]


def _s5(u2d, tt, ws, wo, a16, dvec, layer, bsz, ctx_len):
    n_rows, width = u2d.shape
    rows = n_rows // S5_Q
    n_sg = width // LANES
    w0 = layer * n_sg
    n_blk = 4
    rows_blk = rows // n_blk
    assert rows_blk % 8 == 0
    n_tiles = rows // 8
    ctx_tiles = (ctx_len // S5_Q) * bsz // 8
    pairs_sg = S5_PAIRS // n_sg
    pw = 2 * S5_Q * S5_CH
    return pl.pallas_call(
        functools.partial(_s5_kernel, rows_blk=rows_blk, n_tiles=n_tiles, ctx_tiles=ctx_tiles),
        grid=(n_sg, 2, n_blk),
        in_specs=[
            pl.BlockSpec((rows_blk * S5_Q, LANES), lambda s, ph, rb: (rb, s)),
            pl.BlockSpec((1, 2 * pairs_sg, pw // 2, pw // 2), lambda s, ph, rb: (w0 + s, 0, 0, 0)),
            pl.BlockSpec((1, pairs_sg, pw, pw), lambda s, ph, rb: (w0 + s, 0, 0, 0)),
            pl.BlockSpec((1, pairs_sg, pw, pw), lambda s, ph, rb: (w0 + s, 0, 0, 0)),
            pl.BlockSpec((1, pairs_sg, 8, LANES), lambda s, ph, rb: (w0 + s, 0, 0, 0)),
            pl.BlockSpec((1, 1, LANES), lambda s, ph, rb: (w0 + s, 0, 0)),
        ],
        out_specs=pl.BlockSpec((rows_blk * S5_Q, LANES), lambda s, ph, rb: (ph * rb, s)),
        out_shape=jax.ShapeDtypeStruct(u2d.shape, F32),
        scratch_shapes=[pltpu.VMEM((rows, pairs_sg * pw), F32), pltpu.VMEM((rows, pairs_sg * pw), BF16)],
        compiler_params=pltpu.CompilerParams(dimension_semantics=("arbitrary", "arbitrary", "arbitrary"),
                                             vmem_limit_bytes=VMEM_LIMIT),
        name="s5",
    )(u2d, tt, ws, wo, a16, dvec)


def _cmul(ar, ai, br, bi):
    return ar * br - ai * bi, ar * bi + ai * br


def _cpow(br, bi, expo, nbits):
    rr = jnp.ones_like(br)
    ri = jnp.zeros_like(br)
    for bit in range(nbits):
        take = jnp.bitwise_and(lax.shift_right_logical(expo, bit), 1) == 1
        nr, ni = _cmul(rr, ri, br, bi)
        rr = jnp.where(take, nr, rr)
        ri = jnp.where(take, ni, ri)
        if bit + 1 < nbits:
            br, bi = _cmul(br, bi, br, bi)
    return rr, ri


def _s5_prep_kernel(rows_ref, lcol_ref, bt_ref, ct_ref, tt_ref, ws_ref, wo_ref, a_ref):
    q, ch, p = S5_Q, S5_CH, S5_STATE
    gw, pw = q * ch, 2 * q * ch
    lam_row = [(rows_ref[0, 2 * d:2 * d + 1, :], rows_ref[0, 2 * d + 1:2 * d + 2, :]) for d in range(2)]
    f_row = [(rows_ref[0, 4 + 2 * d:5 + 2 * d, :], rows_ref[0, 5 + 2 * d:6 + 2 * d, :]) for d in range(2)]
    bt_r, bt_i = bt_ref[0, 0], bt_ref[0, 1]
    ct_r, ct_i = ct_ref[0, 0], ct_ref[0, 1]

    ri = lax.broadcasted_iota(jnp.int32, (pw, 2 * p), 0)
    li = lax.broadcasted_iota(jnp.int32, (pw, 2 * p), 1)
    same = lax.shift_right_logical(ri, 8) == lax.shift_right_logical(li, 6)
    b2_r = jnp.where(same, jnp.concatenate([bt_r] * (pw // ch), axis=0), 0.0)
    b2_i = jnp.where(same, jnp.concatenate([bt_i] * (pw // ch), axis=0), 0.0)
    blocks = []
    for d in range(2):
        pows = [f_row[d]]
        for _ in range(q - 1):
            pows.append(_cmul(*pows[-1], *lam_row[d]))
        order = [q - 1 - jb if d == 0 else jb for jb in range(q)] * 2
        gr = jnp.concatenate([jnp.broadcast_to(pows[t][0], (ch, 2 * p)) for t in order], axis=0)
        gi = jnp.concatenate([jnp.broadcast_to(pows[t][1], (ch, 2 * p)) for t in order], axis=0)
        wr, wi = _cmul(b2_r, b2_i, gr, gi)
        blocks += [wr, wi]
    ws_ref[0] = jnp.concatenate(blocks, axis=1).astype(BF16)

    ri = lax.broadcasted_iota(jnp.int32, (2 * p, pw), 0)
    li = lax.broadcasted_iota(jnp.int32, (2 * p, pw), 1)
    row_g = lax.shift_right_logical(ri, 6)
    slot = lax.shift_right_logical(li, 4)
    same = row_g == lax.shift_right_logical(li, 8)
    i = jnp.bitwise_and(slot, q - 1)
    lam_col = [(jnp.concatenate([lcol_ref[0, 2 * d]] * (pw // LANES), axis=1),
                jnp.concatenate([lcol_ref[0, 2 * d + 1]] * (pw // LANES), axis=1)) for d in range(2)]
    fr, fi = _cpow(lam_col[0][0], lam_col[0][1], i, 4)
    fr, fi = _cmul(fr, fi, lam_col[0][0], lam_col[0][1])
    br, bi = _cpow(lam_col[1][0], lam_col[1][1], q - 1 - i, 4)
    br1, bi1 = _cmul(br, bi, lam_col[1][0], lam_col[1][1])
    blocks = []
    for pr, pi in ((fr, fi), (br1, bi1)):
        cr, ci = _cmul(ct_r, ct_i, pr, pi)
        blocks += [jnp.where(same, cr, 0.0), jnp.where(same, -ci, 0.0)]
    wo_ref[0] = jnp.concatenate(blocks, axis=0).astype(BF16)

    lag0 = slot == q - 1
    mf = _cmul(ct_r, ct_i, jnp.where(lag0, 1.0, fr), jnp.where(lag0, 0.0, fi))
    mb = _cmul(ct_r, ct_i, br, bi)
    use_f = jnp.logical_and(slot >= q - 1, slot <= 2 * q - 2)
    use_b = slot <= q - 1
    bb = [_cmul(bt_r, bt_i, f_row[d][0], f_row[d][1]) for d in range(2)]
    lhs = jnp.concatenate([bb[0][0], bb[0][1], bb[1][0], bb[1][1]], axis=1)
    for g in range(2):
        mine = row_g == g
        rhs = jnp.concatenate([jnp.where(jnp.logical_and(mine, use_f), mf[0], 0.0),
                               jnp.where(jnp.logical_and(mine, use_f), -mf[1], 0.0),
                               jnp.where(jnp.logical_and(mine, use_b), mb[0], 0.0),
                               jnp.where(jnp.logical_and(mine, use_b), -mb[1], 0.0)], axis=0)
        z = jnp.dot(lhs, rhs, preferred_element_type=F32, precision=lax.Precision.HIGHEST)
        for jb in range(q):
            shift = (pw - ch * (q - 1 - jb)) % pw
            win = z if shift == 0 else pltpu.roll(z, shift, 1)
            tt_ref[0, g, jb * ch:(jb + 1) * ch, :] = win[:, 0:gw].astype(BF16)

    out = []
    for d in range(2):
        ar, ai = lam_row[d]
        for _ in range(4):
            ar, ai = _cmul(ar, ai, ar, ai)
        out += [ar, ai]
    a_ref[0] = jnp.concatenate(out + [jnp.zeros((4, 2 * p), F32)], axis=0)


def _s5_weights(a_re, a_im, log_dt, b_re, b_im, c_re, c_im):
    depth = a_re.shape[0]
    q, ch, p, npair = S5_Q, S5_CH, S5_STATE, S5_PAIRS
    lr, li = a_re.astype(F32), a_im.astype(F32)
    step = jnp.exp(log_dt.astype(F32))[..., None]
    mag = jnp.exp(lr * step)
    lbr, lbi = mag * jnp.cos(li * step), mag * jnp.sin(li * step)
    den = lr * lr + li * li
    fr, fi = ((lbr - 1.0) * lr + lbi * li) / den, (lbi * lr - (lbr - 1.0) * li) / den

    def pair_rows(v):
        return v.reshape(depth, 2, npair, 2 * p).transpose(0, 2, 1, 3).reshape(depth * npair, 2, 2 * p)

    vals = [pair_rows(v) for v in (lbr, lbi, fr, fi)]
    rows = jnp.stack([vals[0][:, 0], vals[1][:, 0], vals[0][:, 1], vals[1][:, 1],
                      vals[2][:, 0], vals[3][:, 0], vals[2][:, 1], vals[3][:, 1]], axis=1)
    lcol = jnp.broadcast_to(rows[:, 0:4, :, None], (depth * npair, 4, 2 * p, LANES))

    def b_t(b):
        return b.astype(F32).reshape(depth * npair, 2, p, ch).transpose(0, 3, 1, 2).reshape(depth * npair, ch, 2 * p)

    def c_t(c):
        ct = c.astype(F32).reshape(depth * npair, 2, ch, p).transpose(0, 1, 3, 2).reshape(depth * npair, 2 * p, ch)
        return jnp.tile(ct, (1, 1, 2 * q))

    bt = jnp.stack([b_t(b_re), b_t(b_im)], axis=1)
    ct = jnp.stack([c_t(c_re), c_t(c_im)], axis=1)
    n = depth * npair
    gw, pw = q * ch, 2 * q * ch
    return pl.pallas_call(
        _s5_prep_kernel,
        grid=(n,),
        in_specs=[
            pl.BlockSpec((1, 8, 2 * p), lambda i: (i, 0, 0)),
            pl.BlockSpec((1, 4, 2 * p, LANES), lambda i: (i, 0, 0, 0)),
            pl.BlockSpec((1, 2, ch, 2 * p), lambda i: (i, 0, 0, 0)),
            pl.BlockSpec((1, 2, 2 * p, pw), lambda i: (i, 0, 0, 0)),
        ],
        out_specs=[
            pl.BlockSpec((1, 2, gw, gw), lambda i: (i, 0, 0, 0)),
            pl.BlockSpec((1, pw, pw), lambda i: (i, 0, 0)),
            pl.BlockSpec((1, pw, pw), lambda i: (i, 0, 0)),
            pl.BlockSpec((1, 8, 2 * p), lambda i: (i, 0, 0)),
        ],
        out_shape=[
            jax.ShapeDtypeStruct((n, 2, gw, gw), BF16),
            jax.ShapeDtypeStruct((n, pw, pw), BF16),
            jax.ShapeDtypeStruct((n, pw, pw), BF16),
            jax.ShapeDtypeStruct((n, 8, 2 * p), F32),
        ],
        compiler_params=pltpu.CompilerParams(vmem_limit_bytes=VMEM_LIMIT),
        name="s5_prep",
    )(rows, lcol, bt, ct)


def _tail_kernel(*refs, period, aliased):
    (x_ref, yf_ref, yb_ref, xs_ref, z_ref, y5_ref, mod_ref, dv_ref, ng_ref, gw_ref, gb_ref,
     wo_ref, gpm_ref, gpf_ref, wu_ref, cw_ref, cb_ref, wd_ref, gqf_ref) = refs[:19]
    o_ref, act_ref = refs[-2:]
    assert len(refs) == 21 + int(aliased)
    tm = x_ref.shape[1]
    d = D_MODEL
    mod = mod_ref[0]
    gate1, shift2, scale2, gate2 = (mod[:, 2 * d:3 * d], mod[:, 3 * d:4 * d],
                                    mod[:, 4 * d:5 * d], mod[:, 5 * d:6 * d])

    y_ssd = (yf_ref[0] + yb_ref[0] + xs_ref[0] * dv_ref[...]) * _silu(z_ref[0])
    y_ssd = _rms(y_ssd, ng_ref[...])
    g5 = _gelu_tanh(y5_ref[...].reshape(tm, D_S5))
    y_s5 = g5 * _sigmoid(_dot(g5.astype(BF16), gw_ref[...]) + gb_ref[...])
    mix = _dot(y_ssd.astype(BF16), wo_ref[0:D_SSD, :]) + _dot(y_s5.astype(BF16), wo_ref[D_SSD:, :])
    x1 = x_ref[0] + gate1 * _rms(mix, gpm_ref[...])

    h2 = (_rms(x1, gpf_ref[...]) * (1.0 + scale2) + shift2).astype(BF16)
    run = period + 1
    r8 = lax.broadcasted_iota(jnp.int32, (8, FFN_COLS), 0)

    def zero_edge(a, row):
        pieces = []
        for s in range(0, tm, run):
            if row == 0:
                pieces += [jnp.where(r8 == 0, 0.0, a[s:s + 8]), a[s + 8:s + run]]
            else:
                pieces += [a[s:s + run - 8], jnp.where(r8 == 7, 0.0, a[s + run - 8:s + run])]
        return jnp.concatenate(pieces, axis=0)

    def conv(v, col):
        left = zero_edge(pltpu.roll(v, 1, 0), 0)
        right = zero_edge(pltpu.roll(v, tm - 1, 0), 7)
        w = cw_ref[:, col:col + FFN_COLS]
        return left * w[0:1, :] + v * w[1:2, :] + right * w[2:3, :] + cb_ref[:, col:col + FFN_COLS]

    for c in range(D_FF // FFN_COLS):
        c0 = c * FFN_COLS
        gate = conv(_dot(h2, wu_ref[:, c0:c0 + FFN_COLS]), c0)
        val = conv(_dot(h2, wu_ref[:, D_FF + c0:D_FF + c0 + FFN_COLS]), D_FF + c0)
        act_ref[:, c0:c0 + FFN_COLS] = (_silu(gate) * val).astype(BF16)
    o_ref[0] = x1 + gate2 * _rms(_dot(act_ref[...], wd_ref[...]), gqf_ref[...])


def _tail(xa, ca, ctx_blk, yf, yb, xbc, z, y5, mod, layer, p, seq, ctx_len, with_context):
    bsz = xa.shape[0]
    t_out = seq + (ctx_len if with_context else 0)
    weights = [
        _full((1, D_SSD)), _full((1, D_SSD)),
        _resident((D_S5, D_S5), layer), _full((1, D_S5)),
        _resident((D_MODEL, D_MODEL), layer),
        _full((1, D_MODEL)), _full((1, D_MODEL)),
        _resident((D_MODEL, 2 * D_FF), layer),
        _full((3, 2 * D_FF)), _full((1, 2 * D_FF)),
        _resident((D_FF, D_MODEL), layer),
        _full((1, D_MODEL)),
    ]
    w_args = (p['ssd_d'], p['ssd_norm_g'], p['glu_w'], p['glu_b'], p['w_out'], p['g_post_mix'],
              p['g_pre_ffn'], p['w_up'], p['ffn_conv_w'], p['ffn_conv_b'], p['w_down'], p['g_post_ffn'])
    out_shape = jax.ShapeDtypeStruct((bsz, t_out, D_MODEL), F32)
    params = pltpu.CompilerParams(vmem_limit_bytes=VMEM_LIMIT)

    tm = LATENT_TILE
    tok = lambda w: pl.BlockSpec((1, tm, w), lambda b, t: (b, t, 0))
    out = pl.pallas_call(
        functools.partial(_tail_kernel, period=GRID_W - 1, aliased=False),
        grid=(bsz, seq // tm),
        in_specs=[
            tok(D_MODEL), tok(D_SSD), tok(D_SSD), tok(D_SSD), tok(D_SSD),
            pl.BlockSpec((tm // S5_Q, S5_Q, D_S5), lambda b, t: (t, b, 0)),
            pl.BlockSpec((1, 1, 6 * D_MODEL), lambda b, t: (layer * MOD_ROWS + b, 0, 0)),
        ] + weights,
        out_specs=tok(D_MODEL),
        out_shape=out_shape,
        scratch_shapes=[pltpu.VMEM((tm, D_FF), BF16)],
        compiler_params=params,
        name="tail",
    )(xa, yf, yb, xbc, z, y5, mod, *w_args)
    if not with_context:
        return out

    tc = ctx_len
    c0 = seq // tc
    ctok = lambda w: pl.BlockSpec((1, tc, w), lambda b, t: (b, c0, 0))
    return pl.pallas_call(
        functools.partial(_tail_kernel, period=tc - 1, aliased=True),
        grid=(bsz, 1),
        in_specs=[
            pl.BlockSpec((1, tc, D_MODEL), lambda b, t: (b, ctx_blk, 0)),
            ctok(D_SSD), ctok(D_SSD), ctok(D_SSD), ctok(D_SSD),
            pl.BlockSpec((tc // S5_Q, S5_Q, D_S5), lambda b, t: (c0, b, 0)),
            pl.BlockSpec((1, 1, 6 * D_MODEL), lambda b, t: (layer * MOD_ROWS + bsz, 0, 0)),
        ] + weights + [pl.BlockSpec(memory_space=pl.ANY)],
        out_specs=ctok(D_MODEL),
        out_shape=out_shape,
        scratch_shapes=[pltpu.VMEM((tc, D_FF), BF16)],
        input_output_aliases={19: 0},
        compiler_params=params,
        name="tail_ctx",
    )(ca, yf, yb, xbc, z, y5, mod, *w_args, out)


def _row(v, width=None):
    v = v.astype(F32).reshape(1, -1)
    if width is not None and v.shape[1] < width:
        v = jnp.pad(v, ((0, 0), (0, width - v.shape[1])))
    return v


def kernel(x, c, ctx, c_ctx, w_ada, b_ada, g_pre_mix, g_post_mix, g_pre_ffn, g_post_ffn, w_in, ssd_conv_w,
           ssd_conv_b, ssd_dt_bias, ssd_a_log, ssd_d, ssd_norm_g, s5_a_re, s5_a_im, s5_log_dt, s5_b_re,
           s5_b_im, s5_c_re, s5_c_im, s5_d, s5_glu_w, s5_glu_b, w_out, ffn_w_up, ffn_conv_w, ffn_conv_b,
           ffn_w_down):
    bsz, seq, _ = x.shape
    ctx_len = ctx.shape[1]
    depth = w_in.shape[0]
    assert seq % LATENT_TILE == 0 and seq % ctx_len == 0 and ctx_len % SSD_CHUNK == 0
    assert bsz + 1 <= MOD_ROWS and bsz * 2 == 8

    cvec = jnp.concatenate([c, c_ctx[None, :], jnp.zeros((MOD_ROWS - bsz - 1, D_MODEL), F32)], axis=0)
    mod = _ada_table(cvec, w_ada, b_ada).reshape(depth * MOD_ROWS, 1, 6 * D_MODEL)
    xa, ca, ctx_blk = x, ctx, 0

    n_sg = D_S5 // LANES
    psg = S5_PAIRS // n_sg
    gw = S5_Q * S5_CH
    tt, ws, wo, a16 = _s5_weights(s5_a_re, s5_a_im, s5_log_dt, s5_b_re, s5_b_im, s5_c_re, s5_c_im)
    tt = tt.reshape(depth * n_sg, 2 * psg, gw, gw)
    ws = ws.reshape(depth * n_sg, psg, 2 * gw, 2 * gw)
    wo = wo.reshape(depth * n_sg, psg, 2 * gw, 2 * gw)
    a16 = a16.reshape(depth * n_sg, psg, 8, LANES)
    dvec = s5_d.astype(F32).reshape(depth * n_sg, 1, LANES)

    z0, z1, z2 = D_SSD, D_SSD + XBC_DIM, D_SSD + XBC_DIM + 2 * SSD_HEADS
    w_in_all = jnp.concatenate([w_in[:, :, :z1], w_in[:, :, z2:], w_in[:, :, z1:z2],
                                jnp.zeros((depth, D_MODEL, LANES - 2 * SSD_HEADS), w_in.dtype)],
                               axis=2).astype(BF16)
    glu_all, w_out_all = s5_glu_w.astype(BF16), w_out.astype(BF16)
    w_up_all, w_down_all = ffn_w_up.astype(BF16), ffn_w_down.astype(BF16)
    for l in range(depth):
        last = l == depth - 1
        z, xbc, dt, u = _inproj(xa, ca, ctx_blk, mod, l, _row(g_pre_mix[l]), w_in_all, ssd_conv_w[l].astype(F32),
                                _row(ssd_conv_b[l]), _row(ssd_dt_bias[l], LANES), seq, ctx_len)
        yf, yb = _ssd(xbc, dt, _row(ssd_a_log[l], LANES), ctx_len)
        y5 = _s5(u.reshape(-1, D_S5), tt, ws, wo, a16, dvec, l, bsz, ctx_len).reshape(u.shape)
        p = {
            'ssd_d': _row(jnp.repeat(ssd_d[l], SSD_HEAD_DIM)), 'ssd_norm_g': _row(ssd_norm_g[l]),
            'glu_w': glu_all, 'glu_b': _row(s5_glu_b[l]), 'w_out': w_out_all,
            'g_post_mix': _row(g_post_mix[l]), 'g_pre_ffn': _row(g_pre_ffn[l]),
            'w_up': w_up_all, 'ffn_conv_w': ffn_conv_w[l].astype(F32),
            'ffn_conv_b': _row(ffn_conv_b[l]), 'w_down': w_down_all,
            'g_post_ffn': _row(g_post_ffn[l]),
        }
        xa = _tail(xa, ca, ctx_blk, yf, yb, xbc, z, y5, mod, l, p, seq, ctx_len, with_context=not last)
        ca, ctx_blk = xa, seq // ctx_len
    return xa
```

```python
import functools

import jax
import jax.numpy as jnp
from jax import lax
from jax.experimental import pallas as pl
from jax.experimental.pallas import tpu as pltpu

F32 = jnp.float32
BF16 = jnp.bfloat16

D_MODEL = 1024
D_SSD = 512
SSD_HEADS = 8
SSD_HEAD_DIM = 64
SSD_GROUPS = 2
SSD_STATE = 128
SSD_CHUNK = 128
SSD_BLOCK = 256
XBC_DIM = D_SSD + 2 * SSD_GROUPS * SSD_STATE
D_S5 = 512
S5_CH = 16
S5_GROUPS = 32
S5_STATE = 64
S5_Q = 16
S5_PAIRS = S5_GROUPS // 2
D_FF = 2816
FFN_COLS = 256
GRID_W = 64
NORM_EPS = 1e-6
LOG2_E = 1.4426950408889634
LANES = 128
LATENT_TILE = 512
W_IN_COLS = D_SSD + XBC_DIM + D_S5 + LANES
MOD_ROWS = 8
VMEM_LIMIT = 56 * 1024 * 1024


def _silu(x):
    return x * (1.0 / (1.0 + jnp.exp(-x)))


def _sigmoid(x):
    return 1.0 / (1.0 + jnp.exp(-x))


def _gelu_tanh(x):
    c = 0.7978845608028654
    return x * (0.5 * (1.0 + jnp.tanh(c * (x + 0.044715 * (x * x * x)))))


def _softplus(x):
    return jnp.maximum(x, 0.0) + jnp.log(1.0 + jnp.exp(-jnp.abs(x)))


def _rms(x, g):
    ms = jnp.mean(x * x, axis=-1, keepdims=True)
    return x * lax.rsqrt(ms + NORM_EPS) * g


def _dot(a, b):
    return jnp.dot(a, b, preferred_element_type=F32)


def _full(shape):
    n = len(shape)
    return pl.BlockSpec(shape, lambda *_: (0,) * n)


def _layer(shape, layer):
    n = len(shape)
    return pl.BlockSpec((None,) + tuple(shape), lambda *_: (layer,) + (0,) * n)


def _resident(shape, layer):
    n = len(shape)
    return pl.BlockSpec((None,) + tuple(shape), lambda *_: (layer,) + (0,) * n, pipeline_mode=pl.Buffered(1))


def _ada_kernel(c_ref, w_ref, b_ref, o_ref):
    c = _silu(c_ref[...]).astype(BF16)
    o_ref[0] = _dot(c, w_ref[0].astype(BF16)) + b_ref[0]


def _ada_table(cvec, w_ada, b_ada):
    depth = w_ada.shape[0]
    n_col = 6 * D_MODEL // D_MODEL
    return pl.pallas_call(
        _ada_kernel,
        grid=(depth, n_col),
        in_specs=[
            _full((MOD_ROWS, D_MODEL)),
            pl.BlockSpec((1, D_MODEL, D_MODEL), lambda l, n: (l, 0, n)),
            pl.BlockSpec((1, 1, D_MODEL), lambda l, n: (l, 0, n)),
        ],
        out_specs=pl.BlockSpec((1, MOD_ROWS, D_MODEL), lambda l, n: (l, 0, n)),
        out_shape=jax.ShapeDtypeStruct((depth, MOD_ROWS, 6 * D_MODEL), F32),
        name="ada_table",
    )(cvec, w_ada, b_ada.reshape(depth, 1, 6 * D_MODEL))


def _inproj_kernel(*refs, halo):
    if halo:
        (x_ref, xp_ref, xn_ref, mod_ref, g_ref, w_ref, cw_ref, cb_ref, dtb_ref,
         z_ref, xbc_ref, dt_ref, u_ref, h_ref) = refs
    else:
        (x_ref, mod_ref, g_ref, w_ref, cw_ref, cb_ref, dtb_ref, _, _, _, _,
         z_ref, xbc_ref, dt_ref, u_ref, h_ref) = refs
    t = pl.program_id(1)
    nt = pl.num_programs(1)
    tm = x_ref.shape[1]
    x_tile = x_ref[0]
    g = g_ref[...]
    mod = mod_ref[0]
    shift = mod[:, 0:D_MODEL]
    scale = mod[:, D_MODEL:2 * D_MODEL]

    def prenorm(xv):
        return (_rms(xv, g) * (1.0 + scale) + shift).astype(BF16)

    h_ref[0:tm, :] = prenorm(x_tile)
    c_xbc, c_u, c_dt = D_SSD, D_SSD + XBC_DIM, D_SSD + XBC_DIM + D_S5

    if halo:
        h_ref[tm:tm + 8, :] = prenorm(xp_ref[0])
        h_ref[tm + 8:tm + 16, :] = prenorm(xn_ref[0])
        xbc_all = _dot(h_ref[...], w_ref[:, c_xbc:c_u])
        xbc = xbc_all[0:tm]
        edge_prev = jnp.broadcast_to(xbc_all[tm + 7:tm + 8, :] * (t != 0).astype(F32), (8, XBC_DIM))
        edge_next = jnp.broadcast_to(xbc_all[tm + 8:tm + 9, :] * (t != nt - 1).astype(F32), (8, XBC_DIM))
    else:
        xbc = _dot(h_ref[...], w_ref[:, c_xbc:c_u])
        edge_prev = edge_next = jnp.zeros((8, XBC_DIM), F32)
    z_ref[0] = _dot(h_ref[0:tm, :], w_ref[:, 0:c_xbc])
    u_ref[...] = _dot(h_ref[0:tm, :], w_ref[:, c_u:c_dt]).reshape(tm // S5_Q, S5_Q, D_S5)
    dt_ref[0] = _softplus(_dot(h_ref[0:tm, :], w_ref[:, c_dt:]) + dtb_ref[...])
    r8 = lax.broadcasted_iota(jnp.int32, (8, XBC_DIM), 0)
    left = pltpu.roll(xbc, 1, 0)
    left = jnp.concatenate([jnp.where(r8 == 0, edge_prev, left[0:8]), left[8:]], axis=0)
    right = pltpu.roll(xbc, tm - 1, 0)
    right = jnp.concatenate([right[:tm - 8], jnp.where(r8 == 7, edge_next, right[tm - 8:])], axis=0)
    conv = left * cw_ref[0:1, :] + xbc * cw_ref[1:2, :] + right * cw_ref[2:3, :] + cb_ref[...]
    xbc_ref[0] = _silu(conv)


def _inproj(xa, ca, ctx_blk, mod, layer, g_pre, w_in, conv_w, conv_b, dt_bias, seq, ctx_len):
    bsz = xa.shape[0]
    t_all = seq + ctx_len
    tm = LATENT_TILE
    rows8 = tm // 8
    last8 = seq // 8 - 1
    weights = [_layer((1, D_MODEL), layer), _resident((D_MODEL, W_IN_COLS), layer), _layer((3, XBC_DIM), layer),
               _layer((1, XBC_DIM), layer), _layer((1, LANES), layer)]
    w_args = (g_pre, w_in, conv_w, conv_b, dt_bias)
    out_shape = [
        jax.ShapeDtypeStruct((bsz, t_all, D_SSD), F32),
        jax.ShapeDtypeStruct((bsz, t_all, XBC_DIM), F32),
        jax.ShapeDtypeStruct((bsz, t_all, LANES), F32),
        jax.ShapeDtypeStruct((t_all // S5_Q, bsz * S5_Q, D_S5), F32),
    ]
    params = pltpu.CompilerParams(vmem_limit_bytes=VMEM_LIMIT)

    tok = lambda w: pl.BlockSpec((1, tm, w), lambda b, t: (b, t, 0))
    outs = pl.pallas_call(
        functools.partial(_inproj_kernel, halo=True),
        grid=(bsz, seq // tm),
        in_specs=[
            tok(D_MODEL),
            pl.BlockSpec((1, 8, D_MODEL), lambda b, t: (b, jnp.maximum(t * rows8 - 1, 0), 0)),
            pl.BlockSpec((1, 8, D_MODEL), lambda b, t: (b, jnp.minimum((t + 1) * rows8, last8), 0)),
            pl.BlockSpec((1, 1, 6 * D_MODEL), lambda b, t: (layer * MOD_ROWS + b, 0, 0)),
        ] + weights,
        out_specs=[tok(D_SSD), tok(XBC_DIM), tok(LANES),
                   pl.BlockSpec((tm // S5_Q, S5_Q, D_S5), lambda b, t: (t, b, 0))],
        out_shape=out_shape,
        scratch_shapes=[pltpu.VMEM((tm + 16, D_MODEL), BF16)],
        compiler_params=params,
        name="inproj",
    )(xa, xa, xa, mod, *w_args)

    tc = ctx_len
    c0 = seq // tc
    ctok = lambda w: pl.BlockSpec((1, tc, w), lambda b, t: (b, c0, 0))
    anyspec = pl.BlockSpec(memory_space=pl.ANY)
    return pl.pallas_call(
        functools.partial(_inproj_kernel, halo=False),
        grid=(bsz, 1),
        in_specs=[
            pl.BlockSpec((1, tc, D_MODEL), lambda b, t: (b, ctx_blk, 0)),
            pl.BlockSpec((1, 1, 6 * D_MODEL), lambda b, t: (layer * MOD_ROWS + bsz, 0, 0)),
        ] + weights + [anyspec] * 4,
        out_specs=[ctok(D_SSD), ctok(XBC_DIM), ctok(LANES),
                   pl.BlockSpec((tc // S5_Q, S5_Q, D_S5), lambda b, t: (c0, b, 0))],
        out_shape=out_shape,
        scratch_shapes=[pltpu.VMEM((tc, D_MODEL), BF16)],
        input_output_aliases={7: 0, 8: 1, 9: 2, 10: 3},
        compiler_params=params,
        name="inproj_ctx",
    )(ca, mod, *w_args, *outs)


def _split3(x):
    hi = x.astype(BF16)
    r1 = x - hi.astype(F32)
    mid = r1.astype(BF16)
    lo = (r1 - mid.astype(F32)).astype(BF16)
    return hi, mid, lo


def _ssd_prologue(d, x_ref, dt_ref, a_row, r0):
    q = SSD_CHUNK
    fwd = d == 0
    ri = lax.broadcasted_iota(jnp.int32, (q, q), 0)
    ci = lax.broadcasted_iota(jnp.int32, (q, q), 1)
    mask = (ri >= ci) if fwd else (ri <= ci)
    end = q - 1 if fwd else 0
    xbc = x_ref[0, r0:r0 + q, :]
    dt = dt_ref[0, r0:r0 + q, :]
    tri = mask.astype(BF16)
    hi, mid, lo = _split3(dt * a_row)
    cum = (_dot(tri, hi) + _dot(tri, mid) + _dot(tri, lo)) * LOG2_E
    bm, cm, cb = [], [], []
    for g in range(SSD_GROUPS):
        bm.append(xbc[:, D_SSD + g * SSD_STATE:D_SSD + (g + 1) * SSD_STATE])
        cm.append(xbc[:, D_SSD + (SSD_GROUPS + g) * SSD_STATE:D_SSD + (SSD_GROUPS + g + 1) * SSD_STATE])
        cb.append(lax.dot_general(cm[g].astype(BF16), bm[g].astype(BF16), (((1,), (1,)), ((), ())),
                                  preferred_element_type=F32))
    return dict(d=d, r0=r0, mask=mask, end=end, cum=cum, cum_t=cum.T, dt_t=dt.T,
                etot=jnp.exp2(cum[end:end + 1, :]), xs_b=xbc[:, 0:D_SSD].astype(BF16),
                cm=cm, cb=cb, bm_t=[b.T for b in bm])


def _ssd_pair(c, pair, st_pair, y_ref):
    q = SSD_CHUNK
    d, r0, end, g = c['d'], c['r0'], c['end'], pair // 2
    lo64 = lax.broadcasted_iota(jnp.int32, (1, LANES), 1) < SSD_HEAD_DIM
    sl = slice(pair * LANES, (pair + 1) * LANES)
    xs_pair = c['xs_b'][:, sl]
    st_pair_b = st_pair.astype(BF16)
    acc = None
    new = None
    for hh in range(2):
        lane = SSD_HEADS * d + 2 * pair + hh
        crow = c['cum_t'][lane:lane + 1, :]
        drow = c['dt_t'][lane:lane + 1, :]
        cl = jnp.broadcast_to(c['cum'][:, lane:lane + 1], (q, q))
        decay = jnp.exp2(jnp.where(c['mask'], cl - crow, -jnp.inf))
        m = c['cb'][g] * decay * drow
        cme = c['cm'][g] * jnp.exp2(cl)
        lhs = jnp.concatenate([m, cme], axis=1).astype(BF16)
        sel = lo64 if hh == 0 else jnp.logical_not(lo64)
        xs_h = jnp.where(sel, xs_pair, jnp.zeros_like(xs_pair))
        rhs = jnp.concatenate([xs_h, jnp.where(sel, st_pair_b, jnp.zeros_like(st_pair_b))], axis=0)
        part = _dot(lhs, rhs)
        acc = part if acc is None else acc + part
        wrow = drow * jnp.exp2(crow[:, end:end + 1] - crow)
        part = _dot((c['bm_t'][g] * wrow).astype(BF16), xs_h)
        new = part if new is None else new + part
    y_ref[0, r0:r0 + q, sl] = acc
    l0 = SSD_HEADS * d + 2 * pair
    etot = c['etot']
    dpair = jnp.where(lo64, jnp.broadcast_to(etot[:, l0:l0 + 1], (1, LANES)),
                      jnp.broadcast_to(etot[:, l0 + 1:l0 + 2], (1, LANES)))
    return st_pair * dpair + new


def _ssd_kernel(xf_ref, dtf_ref, xb_ref, dtb_ref, alog_ref, yf_ref, yb_ref, stf_ref, stb_ref):
    @pl.when(pl.program_id(1) == 0)
    def _():
        stf_ref[...] = jnp.zeros_like(stf_ref)
        stb_ref[...] = jnp.zeros_like(stb_ref)

    a_row = -jnp.exp(alog_ref[...])
    n_sub = xf_ref.shape[1] // SSD_CHUNK
    n_pair = SSD_HEADS // 2
    pf = [_ssd_prologue(0, xf_ref, dtf_ref, a_row, s * SSD_CHUNK) for s in range(n_sub)]
    pb = [_ssd_prologue(1, xb_ref, dtb_ref, a_row, (n_sub - 1 - s) * SSD_CHUNK) for s in range(n_sub)]
    sf = [stf_ref[:, p * LANES:(p + 1) * LANES] for p in range(n_pair)]
    sb = [stb_ref[:, p * LANES:(p + 1) * LANES] for p in range(n_pair)]
    for s in range(n_sub):
        for p in range(n_pair):
            sf[p] = _ssd_pair(pf[s], p, sf[p], yf_ref)
            sb[p] = _ssd_pair(pb[s], p, sb[p], yb_ref)
    for p in range(n_pair):
        stf_ref[:, p * LANES:(p + 1) * LANES] = sf[p]
        stb_ref[:, p * LANES:(p + 1) * LANES] = sb[p]


def _ssd(xbc, dt, a_log_row, layer, ctx_len):
    bsz, t_all, _ = xbc.shape
    q = SSD_BLOCK
    assert ctx_len % q == 0 and t_all % q == 0
    nch = t_all // q
    ncc = ctx_len // q
    nlc = nch - ncc

    def fmap(b, i):
        return (b, jnp.where(i < ncc, nlc + i, i - ncc), 0)

    def bmap(b, i):
        return (b, nch - 1 - i, 0)

    return pl.pallas_call(
        _ssd_kernel,
        grid=(bsz, nch),
        in_specs=[
            pl.BlockSpec((1, q, XBC_DIM), fmap),
            pl.BlockSpec((1, q, LANES), fmap),
            pl.BlockSpec((1, q, XBC_DIM), bmap),
            pl.BlockSpec((1, q, LANES), bmap),
            _layer((1, LANES), layer),
        ],
        out_specs=[pl.BlockSpec((1, q, D_SSD), fmap), pl.BlockSpec((1, q, D_SSD), bmap)],
        out_shape=[jax.ShapeDtypeStruct((bsz, t_all, D_SSD), F32)] * 2,
        scratch_shapes=[pltpu.VMEM((SSD_STATE, D_SSD), F32), pltpu.VMEM((SSD_STATE, D_SSD), F32)],
        compiler_params=pltpu.CompilerParams(dimension_semantics=("arbitrary", "arbitrary"),
                                             vmem_limit_bytes=VMEM_LIMIT),
        name="ssd",
    )(xbc, dt, xbc, dt, a_log_row)


def _block_transpose8(xs):
    lane = lax.broadcasted_iota(jnp.int32, (1, LANES), 1)
    xs = list(xs)
    for d in (4, 2, 1):
        keep = jnp.bitwise_and(lane, S5_CH * d) == 0
        nxt = list(xs)
        for a in range(8):
            if a & d:
                continue
            nxt[a] = jnp.where(keep, xs[a], pltpu.roll(xs[a + d], S5_CH * d, 1))
            nxt[a + d] = jnp.where(keep, pltpu.roll(xs[a], LANES - S5_CH * d, 1), xs[a + d])
        xs = nxt
    return xs


def _s5_kernel(u_ref, tt_ref, ws_ref, wo_ref, a_ref, d_ref, y_ref, s_ref, ub_ref,
               *, rows_blk, n_tiles, ctx_tiles):
    phase = pl.program_id(1)
    rb = pl.program_id(2)
    q = S5_Q
    gw = q * S5_CH
    pw = 2 * gw
    n_pairs = u_ref.shape[1] // (2 * S5_CH)
    r0 = pl.multiple_of(rb * rows_blk, 8)

    def token_rows(ref, j):
        return ref[pl.ds(j, rows_blk, stride=q), :]

    @pl.when(phase == 0)
    def _():
        uj = [token_rows(u_ref, j) for j in range(q)]
        ug = [_block_transpose8(uj[0:8]), _block_transpose8(uj[8:16])]
        for p in range(n_pairs):
            up = jnp.concatenate([ug[0][2 * p], ug[1][2 * p], ug[0][2 * p + 1], ug[1][2 * p + 1]],
                                 axis=1).astype(BF16)
            ub_ref[pl.ds(r0, rows_blk), p * pw:(p + 1) * pw] = up
            s_ref[pl.ds(r0, rows_blk), p * pw:(p + 1) * pw] = _dot(up, ws_ref[0, p])

    lo = lax.broadcasted_iota(jnp.int32, (8, LANES), 0) < 4
    hi = jnp.logical_not(lo)

    def roll4(v):
        return pltpu.roll(v, 4, 0)

    def tile_step(kt, cre, cim, a, a2, col, first):
        r0 = pl.multiple_of(kt * 8, 8)
        (ar, ai), (a2r, a2i) = a, a2
        sre = s_ref[pl.ds(r0, 8), col:col + LANES]
        sim = s_ref[pl.ds(r0, 8), col + LANES:col + 2 * LANES]
        swr, swi = roll4(sre), roll4(sim)
        gre = ar * swr - ai * swi + sre
        gim = ar * swi + ai * swr + sim
        s_ref[pl.ds(r0, 8), col:col + LANES] = jnp.where(first, roll4(cre), ar * cre - ai * cim + swr)
        s_ref[pl.ds(r0, 8), col + LANES:col + 2 * LANES] = jnp.where(first, roll4(cim), ar * cim + ai * cre + swi)
        return a2r * cre - a2i * cim + gre, a2r * cim + a2i * cre + gim

    @pl.when(jnp.logical_and(phase == 1, rb == 0))
    def _():
        def arow(p, r):
            return jnp.broadcast_to(a_ref[0, p, r:r + 1, :], (8, LANES))

        coef = []
        for p in range(n_pairs):
            af, ab = (arow(p, 0), arow(p, 1)), (arow(p, 2), arow(p, 3))
            coef.append((af, _cmul(*af, *af), ab, _cmul(*ab, *ab)))

        def body(i, carry):
            kf = jnp.where(i < ctx_tiles, n_tiles - ctx_tiles + i, i - ctx_tiles)
            kb = n_tiles - 1 - i
            out = []
            for p in range(n_pairs):
                fre, fim, bre, bim = carry[4 * p:4 * p + 4]
                af, af2, ab, ab2 = coef[p]
                fre, fim = tile_step(kf, fre, fim, af, af2, p * pw, lo)
                bre, bim = tile_step(kb, bre, bim, ab, ab2, p * pw + 2 * LANES, hi)
                out += [fre, fim, bre, bim]
            return tuple(out)

        zero = jnp.zeros((8, LANES), F32)
        lax.fori_loop(0, n_tiles, body, (zero,) * (4 * n_pairs))

    @pl.when(phase == 1)
    def _():
        yg = [[None] * 8, [None] * 8]
        for p in range(n_pairs):
            up = ub_ref[pl.ds(r0, rows_blk), p * pw:(p + 1) * pw]
            h = s_ref[pl.ds(r0, rows_blk), p * pw:(p + 1) * pw].astype(BF16)
            off = _dot(h, wo_ref[0, p])
            for gi in range(2):
                g = 2 * p + gi
                yp = _dot(up[:, gi * gw:(gi + 1) * gw], tt_ref[0, g]) + off[:, gi * gw:(gi + 1) * gw]
                yg[0][g] = yp[:, 0:LANES]
                yg[1][g] = yp[:, LANES:2 * LANES]
        for half in range(2):
            yi = _block_transpose8(yg[half])
            for il in range(8):
                i = 8 * half + il
                y_ref[pl.ds(i, rows_blk, stride=q), :] = yi[il] + token_rows(u_ref, i) * d_ref[0]


def _s5(u2d, tt, ws, wo, a16, dvec, layer, bsz, ctx_len):
    n_rows, width = u2d.shape
    rows = n_rows // S5_Q
    n_sg = width // LANES
    w0 = layer * n_sg
    n_blk = 4
    rows_blk = rows // n_blk
    assert rows_blk % 8 == 0
    n_tiles = rows // 8
    ctx_tiles = (ctx_len // S5_Q) * bsz // 8
    pairs_sg = S5_PAIRS // n_sg
    pw = 2 * S5_Q * S5_CH
    return pl.pallas_call(
        functools.partial(_s5_kernel, rows_blk=rows_blk, n_tiles=n_tiles, ctx_tiles=ctx_tiles),
        grid=(n_sg, 2, n_blk),
        in_specs=[
            pl.BlockSpec((rows_blk * S5_Q, LANES), lambda s, ph, rb: (rb, s)),
            pl.BlockSpec((1, 2 * pairs_sg, pw // 2, pw // 2), lambda s, ph, rb: (w0 + s, 0, 0, 0)),
            pl.BlockSpec((1, pairs_sg, pw, pw), lambda s, ph, rb: (w0 + s, 0, 0, 0)),
            pl.BlockSpec((1, pairs_sg, pw, pw), lambda s, ph, rb: (w0 + s, 0, 0, 0)),
            pl.BlockSpec((1, pairs_sg, 8, LANES), lambda s, ph, rb: (w0 + s, 0, 0, 0)),
            pl.BlockSpec((1, 1, LANES), lambda s, ph, rb: (w0 + s, 0, 0)),
        ],
        out_specs=pl.BlockSpec((rows_blk * S5_Q, LANES), lambda s, ph, rb: (ph * rb, s)),
        out_shape=jax.ShapeDtypeStruct(u2d.shape, F32),
        scratch_shapes=[pltpu.VMEM((rows, pairs_sg * pw), F32), pltpu.VMEM((rows, pairs_sg * pw), BF16)],
        compiler_params=pltpu.CompilerParams(dimension_semantics=("arbitrary", "arbitrary", "arbitrary"),
                                             vmem_limit_bytes=VMEM_LIMIT),
        name="s5",
    )(u2d, tt, ws, wo, a16, dvec)


def _cmul(ar, ai, br, bi):
    return ar * br - ai * bi, ar * bi + ai * br


def _cpow(br, bi, expo, nbits):
    rr = jnp.ones_like(br)
    ri = jnp.zeros_like(br)
    for bit in range(nbits):
        take = jnp.bitwise_and(lax.shift_right_logical(expo, bit), 1) == 1
        nr, ni = _cmul(rr, ri, br, bi)
        rr = jnp.where(take, nr, rr)
        ri = jnp.where(take, ni, ri)
        if bit + 1 < nbits:
            br, bi = _cmul(br, bi, br, bi)
    return rr, ri


def _s5_prep_kernel(rows_ref, lcol_ref, bt_ref, ct_ref, tt_ref, ws_ref, wo_ref, a_ref):
    q, ch, p = S5_Q, S5_CH, S5_STATE
    gw, pw = q * ch, 2 * q * ch
    lam_row = [(rows_ref[0, 2 * d:2 * d + 1, :], rows_ref[0, 2 * d + 1:2 * d + 2, :]) for d in range(2)]
    f_row = [(rows_ref[0, 4 + 2 * d:5 + 2 * d, :], rows_ref[0, 5 + 2 * d:6 + 2 * d, :]) for d in range(2)]
    bt_r, bt_i = bt_ref[0, 0], bt_ref[0, 1]
    ct_r, ct_i = ct_ref[0, 0], ct_ref[0, 1]

    ri = lax.broadcasted_iota(jnp.int32, (pw, 2 * p), 0)
    li = lax.broadcasted_iota(jnp.int32, (pw, 2 * p), 1)
    same = lax.shift_right_logical(ri, 8) == lax.shift_right_logical(li, 6)
    b2_r = jnp.where(same, jnp.concatenate([bt_r] * (pw // ch), axis=0), 0.0)
    b2_i = jnp.where(same, jnp.concatenate([bt_i] * (pw // ch), axis=0), 0.0)
    blocks = []
    for d in range(2):
        pows = [f_row[d]]
        for _ in range(q - 1):
            pows.append(_cmul(*pows[-1], *lam_row[d]))
        order = [q - 1 - jb if d == 0 else jb for jb in range(q)] * 2
        gr = jnp.concatenate([jnp.broadcast_to(pows[t][0], (ch, 2 * p)) for t in order], axis=0)
        gi = jnp.concatenate([jnp.broadcast_to(pows[t][1], (ch, 2 * p)) for t in order], axis=0)
        wr, wi = _cmul(b2_r, b2_i, gr, gi)
        blocks += [wr, wi]
    ws_ref[0] = jnp.concatenate(blocks, axis=1).astype(BF16)

    ri = lax.broadcasted_iota(jnp.int32, (2 * p, pw), 0)
    li = lax.broadcasted_iota(jnp.int32, (2 * p, pw), 1)
    row_g = lax.shift_right_logical(ri, 6)
    slot = lax.shift_right_logical(li, 4)
    same = row_g == lax.shift_right_logical(li, 8)
    i = jnp.bitwise_and(slot, q - 1)
    lam_col = [(jnp.concatenate([lcol_ref[0, 2 * d]] * (pw // LANES), axis=1),
                jnp.concatenate([lcol_ref[0, 2 * d + 1]] * (pw // LANES), axis=1)) for d in range(2)]
    fr, fi = _cpow(lam_col[0][0], lam_col[0][1], i, 4)
    fr, fi = _cmul(fr, fi, lam_col[0][0], lam_col[0][1])
    br, bi = _cpow(lam_col[1][0], lam_col[1][1], q - 1 - i, 4)
    br1, bi1 = _cmul(br, bi, lam_col[1][0], lam_col[1][1])
    blocks = []
    for pr, pi in ((fr, fi), (br1, bi1)):
        cr, ci = _cmul(ct_r, ct_i, pr, pi)
        blocks += [jnp.where(same, cr, 0.0), jnp.where(same, -ci, 0.0)]
    wo_ref[0] = jnp.concatenate(blocks, axis=0).astype(BF16)

    lag0 = slot == q - 1
    mf = _cmul(ct_r, ct_i, jnp.where(lag0, 1.0, fr), jnp.where(lag0, 0.0, fi))
    mb = _cmul(ct_r, ct_i, br, bi)
    use_f = jnp.logical_and(slot >= q - 1, slot <= 2 * q - 2)
    use_b = slot <= q - 1
    bb = [_cmul(bt_r, bt_i, f_row[d][0], f_row[d][1]) for d in range(2)]
    lhs = jnp.concatenate([bb[0][0], bb[0][1], bb[1][0], bb[1][1]], axis=1)
    for g in range(2):
        mine = row_g == g
        rhs = jnp.concatenate([jnp.where(jnp.logical_and(mine, use_f), mf[0], 0.0),
                               jnp.where(jnp.logical_and(mine, use_f), -mf[1], 0.0),
                               jnp.where(jnp.logical_and(mine, use_b), mb[0], 0.0),
                               jnp.where(jnp.logical_and(mine, use_b), -mb[1], 0.0)], axis=0)
        z = jnp.dot(lhs, rhs, preferred_element_type=F32, precision=lax.Precision.HIGHEST)
        for jb in range(q):
            shift = (pw - ch * (q - 1 - jb)) % pw
            win = z if shift == 0 else pltpu.roll(z, shift, 1)
            tt_ref[0, g, jb * ch:(jb + 1) * ch, :] = win[:, 0:gw].astype(BF16)

    out = []
    for d in range(2):
        ar, ai = lam_row[d]
        for _ in range(4):
            ar, ai = _cmul(ar, ai, ar, ai)
        out += [ar, ai]
    a_ref[0] = jnp.concatenate(out + [jnp.zeros((4, 2 * p), F32)], axis=0)


def _s5_weights(a_re, a_im, log_dt, b_re, b_im, c_re, c_im):
    depth = a_re.shape[0]
    q, ch, p, npair = S5_Q, S5_CH, S5_STATE, S5_PAIRS
    lr, li = a_re.astype(F32), a_im.astype(F32)
    step = jnp.exp(log_dt.astype(F32))[..., None]
    mag = jnp.exp(lr * step)
    lbr, lbi = mag * jnp.cos(li * step), mag * jnp.sin(li * step)
    den = lr * lr + li * li
    fr, fi = ((lbr - 1.0) * lr + lbi * li) / den, (lbi * lr - (lbr - 1.0) * li) / den

    def pair_rows(v):
        return v.reshape(depth, 2, npair, 2 * p).transpose(0, 2, 1, 3).reshape(depth * npair, 2, 2 * p)

    vals = [pair_rows(v) for v in (lbr, lbi, fr, fi)]
    rows = jnp.stack([vals[0][:, 0], vals[1][:, 0], vals[0][:, 1], vals[1][:, 1],
                      vals[2][:, 0], vals[3][:, 0], vals[2][:, 1], vals[3][:, 1]], axis=1)
    lcol = jnp.broadcast_to(rows[:, 0:4, :, None], (depth * npair, 4, 2 * p, LANES))

    def b_t(b):
        return b.astype(F32).reshape(depth * npair, 2, p, ch).transpose(0, 3, 1, 2).reshape(depth * npair, ch, 2 * p)

    def c_t(c):
        ct = c.astype(F32).reshape(depth * npair, 2, ch, p).transpose(0, 1, 3, 2).reshape(depth * npair, 2 * p, ch)
        return jnp.tile(ct, (1, 1, 2 * q))

    bt = jnp.stack([b_t(b_re), b_t(b_im)], axis=1)
    ct = jnp.stack([c_t(c_re), c_t(c_im)], axis=1)
    n = depth * npair
    gw, pw = q * ch, 2 * q * ch
    return pl.pallas_call(
        _s5_prep_kernel,
        grid=(n,),
        in_specs=[
            pl.BlockSpec((1, 8, 2 * p), lambda i: (i, 0, 0)),
            pl.BlockSpec((1, 4, 2 * p, LANES), lambda i: (i, 0, 0, 0)),
            pl.BlockSpec((1, 2, ch, 2 * p), lambda i: (i, 0, 0, 0)),
            pl.BlockSpec((1, 2, 2 * p, pw), lambda i: (i, 0, 0, 0)),
        ],
        out_specs=[
            pl.BlockSpec((1, 2, gw, gw), lambda i: (i, 0, 0, 0)),
            pl.BlockSpec((1, pw, pw), lambda i: (i, 0, 0)),
            pl.BlockSpec((1, pw, pw), lambda i: (i, 0, 0)),
            pl.BlockSpec((1, 8, 2 * p), lambda i: (i, 0, 0)),
        ],
        out_shape=[
            jax.ShapeDtypeStruct((n, 2, gw, gw), BF16),
            jax.ShapeDtypeStruct((n, pw, pw), BF16),
            jax.ShapeDtypeStruct((n, pw, pw), BF16),
            jax.ShapeDtypeStruct((n, 8, 2 * p), F32),
        ],
        compiler_params=pltpu.CompilerParams(vmem_limit_bytes=VMEM_LIMIT),
        name="s5_prep",
    )(rows, lcol, bt, ct)


def _tail_kernel(*refs, period, aliased):
    (x_ref, yf_ref, yb_ref, xs_ref, z_ref, y5_ref, mod_ref, dv_ref, ng_ref, gw_ref, gb_ref,
     wo_ref, gpm_ref, gpf_ref, wu_ref, cw_ref, cb_ref, wd_ref, gqf_ref) = refs[:19]
    o_ref, act_ref = refs[-2:]
    assert len(refs) == 21 + int(aliased)
    tm = x_ref.shape[1]
    d = D_MODEL
    mod = mod_ref[0]
    gate1, shift2, scale2, gate2 = (mod[:, 2 * d:3 * d], mod[:, 3 * d:4 * d],
                                    mod[:, 4 * d:5 * d], mod[:, 5 * d:6 * d])

    y_ssd = (yf_ref[0] + yb_ref[0] + xs_ref[0] * dv_ref[...]) * _silu(z_ref[0])
    y_ssd = _rms(y_ssd, ng_ref[...])
    g5 = _gelu_tanh(y5_ref[...].reshape(tm, D_S5))
    y_s5 = g5 * _sigmoid(_dot(g5.astype(BF16), gw_ref[...]) + gb_ref[...])
    mix = _dot(y_ssd.astype(BF16), wo_ref[0:D_SSD, :]) + _dot(y_s5.astype(BF16), wo_ref[D_SSD:, :])
    x1 = x_ref[0] + gate1 * _rms(mix, gpm_ref[...])

    h2 = (_rms(x1, gpf_ref[...]) * (1.0 + scale2) + shift2).astype(BF16)
    run = period + 1
    r8 = lax.broadcasted_iota(jnp.int32, (8, FFN_COLS), 0)

    def zero_edge(a, row):
        pieces = []
        for s in range(0, tm, run):
            if row == 0:
                pieces += [jnp.where(r8 == 0, 0.0, a[s:s + 8]), a[s + 8:s + run]]
            else:
                pieces += [a[s:s + run - 8], jnp.where(r8 == 7, 0.0, a[s + run - 8:s + run])]
        return jnp.concatenate(pieces, axis=0)

    def conv(v, col):
        left = zero_edge(pltpu.roll(v, 1, 0), 0)
        right = zero_edge(pltpu.roll(v, tm - 1, 0), 7)
        w = cw_ref[:, col:col + FFN_COLS]
        return left * w[0:1, :] + v * w[1:2, :] + right * w[2:3, :] + cb_ref[:, col:col + FFN_COLS]

    for c in range(D_FF // FFN_COLS):
        c0 = c * FFN_COLS
        gate = conv(_dot(h2, wu_ref[:, c0:c0 + FFN_COLS]), c0)
        val = conv(_dot(h2, wu_ref[:, D_FF + c0:D_FF + c0 + FFN_COLS]), D_FF + c0)
        act_ref[:, c0:c0 + FFN_COLS] = (_silu(gate) * val).astype(BF16)
    o_ref[0] = x1 + gate2 * _rms(_dot(act_ref[...], wd_ref[...]), gqf_ref[...])


def _tail(xa, ca, ctx_blk, yf, yb, xbc, z, y5, mod, layer, p, seq, ctx_len, with_context):
    bsz = xa.shape[0]
    t_out = seq + (ctx_len if with_context else 0)
    weights = [
        _layer((1, D_SSD), layer), _layer((1, D_SSD), layer),
        _resident((D_S5, D_S5), layer), _layer((1, D_S5), layer),
        _resident((D_MODEL, D_MODEL), layer),
        _layer((1, D_MODEL), layer), _layer((1, D_MODEL), layer),
        _resident((D_MODEL, 2 * D_FF), layer),
        _layer((3, 2 * D_FF), layer), _layer((1, 2 * D_FF), layer),
        _resident((D_FF, D_MODEL), layer),
        _layer((1, D_MODEL), layer),
    ]
    w_args = (p['ssd_d'], p['ssd_norm_g'], p['glu_w'], p['glu_b'], p['w_out'], p['g_post_mix'],
              p['g_pre_ffn'], p['w_up'], p['ffn_conv_w'], p['ffn_conv_b'], p['w_down'], p['g_post_ffn'])
    out_shape = jax.ShapeDtypeStruct((bsz, t_out, D_MODEL), F32)
    params = pltpu.CompilerParams(vmem_limit_bytes=VMEM_LIMIT)

    tm = LATENT_TILE
    tok = lambda w: pl.BlockSpec((1, tm, w), lambda b, t: (b, t, 0))
    out = pl.pallas_call(
        functools.partial(_tail_kernel, period=GRID_W - 1, aliased=False),
        grid=(bsz, seq // tm),
        in_specs=[
            tok(D_MODEL), tok(D_SSD), tok(D_SSD), tok(D_SSD), tok(D_SSD),
            pl.BlockSpec((tm // S5_Q, S5_Q, D_S5), lambda b, t: (t, b, 0)),
            pl.BlockSpec((1, 1, 6 * D_MODEL), lambda b, t: (layer * MOD_ROWS + b, 0, 0)),
        ] + weights,
        out_specs=tok(D_MODEL),
        out_shape=out_shape,
        scratch_shapes=[pltpu.VMEM((tm, D_FF), BF16)],
        compiler_params=params,
        name="tail",
    )(xa, yf, yb, xbc, z, y5, mod, *w_args)
    if not with_context:
        return out

    tc = ctx_len
    c0 = seq // tc
    ctok = lambda w: pl.BlockSpec((1, tc, w), lambda b, t: (b, c0, 0))
    return pl.pallas_call(
        functools.partial(_tail_kernel, period=tc - 1, aliased=True),
        grid=(bsz, 1),
        in_specs=[
            pl.BlockSpec((1, tc, D_MODEL), lambda b, t: (b, ctx_blk, 0)),
            ctok(D_SSD), ctok(D_SSD), ctok(D_SSD), ctok(D_SSD),
            pl.BlockSpec((tc // S5_Q, S5_Q, D_S5), lambda b, t: (c0, b, 0)),
            pl.BlockSpec((1, 1, 6 * D_MODEL), lambda b, t: (layer * MOD_ROWS + bsz, 0, 0)),
        ] + weights + [pl.BlockSpec(memory_space=pl.ANY)],
        out_specs=ctok(D_MODEL),
        out_shape=out_shape,
        scratch_shapes=[pltpu.VMEM((tc, D_FF), BF16)],
        input_output_aliases={19: 0},
        compiler_params=params,
        name="tail_ctx",
    )(ca, yf, yb, xbc, z, y5, mod, *w_args, out)


def _rows(v, width=None):
    v = v.astype(F32).reshape(v.shape[0], 1, -1)
    if width is not None and v.shape[2] < width:
        v = jnp.pad(v, ((0, 0), (0, 0), (0, width - v.shape[2])))
    return v


def kernel(x, c, ctx, c_ctx, w_ada, b_ada, g_pre_mix, g_post_mix, g_pre_ffn, g_post_ffn, w_in, ssd_conv_w,
           ssd_conv_b, ssd_dt_bias, ssd_a_log, ssd_d, ssd_norm_g, s5_a_re, s5_a_im, s5_log_dt, s5_b_re,
           s5_b_im, s5_c_re, s5_c_im, s5_d, s5_glu_w, s5_glu_b, w_out, ffn_w_up, ffn_conv_w, ffn_conv_b,
           ffn_w_down):
    bsz, seq, _ = x.shape
    ctx_len = ctx.shape[1]
    depth = w_in.shape[0]
    assert seq % LATENT_TILE == 0 and seq % ctx_len == 0 and ctx_len % SSD_CHUNK == 0
    assert bsz + 1 <= MOD_ROWS and bsz * 2 == 8

    cvec = jnp.concatenate([c, c_ctx[None, :], jnp.zeros((MOD_ROWS - bsz - 1, D_MODEL), F32)], axis=0)
    mod = _ada_table(cvec, w_ada, b_ada).reshape(depth * MOD_ROWS, 1, 6 * D_MODEL)
    xa, ca, ctx_blk = x, ctx, 0

    n_sg = D_S5 // LANES
    psg = S5_PAIRS // n_sg
    gw = S5_Q * S5_CH
    tt, ws, wo, a16 = _s5_weights(s5_a_re, s5_a_im, s5_log_dt, s5_b_re, s5_b_im, s5_c_re, s5_c_im)
    tt = tt.reshape(depth * n_sg, 2 * psg, gw, gw)
    ws = ws.reshape(depth * n_sg, psg, 2 * gw, 2 * gw)
    wo = wo.reshape(depth * n_sg, psg, 2 * gw, 2 * gw)
    a16 = a16.reshape(depth * n_sg, psg, 8, LANES)
    dvec = s5_d.astype(F32).reshape(depth * n_sg, 1, LANES)

    z0, z1, z2 = D_SSD, D_SSD + XBC_DIM, D_SSD + XBC_DIM + 2 * SSD_HEADS
    w_in_all = jnp.concatenate([w_in[:, :, :z1], w_in[:, :, z2:], w_in[:, :, z1:z2],
                                jnp.zeros((depth, D_MODEL, LANES - 2 * SSD_HEADS), w_in.dtype)],
                               axis=2).astype(BF16)
    glu_all, w_out_all = s5_glu_w.astype(BF16), w_out.astype(BF16)
    w_up_all, w_down_all = ffn_w_up.astype(BF16), ffn_w_down.astype(BF16)
    g_pre_mix_r, conv_b_r = _rows(g_pre_mix), _rows(ssd_conv_b)
    dt_bias_r, a_log_r = _rows(ssd_dt_bias, LANES), _rows(ssd_a_log, LANES)
    p = {
        'ssd_d': _rows(jnp.repeat(ssd_d, SSD_HEAD_DIM, axis=1)), 'ssd_norm_g': _rows(ssd_norm_g),
        'glu_w': glu_all, 'glu_b': _rows(s5_glu_b), 'w_out': w_out_all,
        'g_post_mix': _rows(g_post_mix), 'g_pre_ffn': _rows(g_pre_ffn),
        'w_up': w_up_all, 'ffn_conv_w': ffn_conv_w.astype(F32),
        'ffn_conv_b': _rows(ffn_conv_b), 'w_down': w_down_all,
        'g_post_ffn': _rows(g_post_ffn),
    }
    for l in range(depth):
        last = l == depth - 1
        z, xbc, dt, u = _inproj(xa, ca, ctx_blk, mod, l, g_pre_mix_r, w_in_all, ssd_conv_w.astype(F32),
                                conv_b_r, dt_bias_r, seq, ctx_len)
        yf, yb = _ssd(xbc, dt, a_log_r, l, ctx_len)
        y5 = _s5(u.reshape(-1, D_S5), tt, ws, wo, a16, dvec, l, bsz, ctx_len).reshape(u.shape)
        xa = _tail(xa, ca, ctx_blk, yf, yb, xbc, z, y5, mod, l, p, seq, ctx_len, with_context=not last)
        ca, ctx_blk = xa, seq // ctx_len
    return xa
```
